```python
import jax, jax.numpy as jnp
from jax import lax
import numpy as np

D_MODEL = 1024
BATCH = 16
SEQ = 2048
DEPTH = 2

GRID_W = 64
CTX_LEN = 256
N_HEADS = 8
N_KV_HEADS = 2
HEAD_DIM = 64
ATTN_WIDTH = N_HEADS * HEAD_DIM
KV_WIDTH = N_KV_HEADS * HEAD_DIM
Q_BLOCK = 128
ROPE_THETA = 10000.0
POOL_WINDOWS = (2, 4, 8, 16)
POOL_GROUPS = 4
POOL_GROUP_DIM = 64
POOL_WIDTH = POOL_GROUPS * POOL_GROUP_DIM
SGU_GROUPS = 4
SGU_GROUP_DIM = 64
SGU_WIDTH = SGU_GROUPS * SGU_GROUP_DIM
CHUNK = 128
N_BRANCHES = 3
D_FF = 2816
EPS = 1e-6
N_MOD = 9
OFF_Q = 0
OFF_K = OFF_Q + ATTN_WIDTH
OFF_V = OFF_K + KV_WIDTH
OFF_POOL = OFF_V + KV_WIDTH
OFF_SGU = OFF_POOL + POOL_WIDTH
OFF_GATE = OFF_SGU + 2 * SGU_WIDTH
IN_COLS = OFF_GATE + N_BRANCHES * D_MODEL

kernel_name = 'hybrid_pool_gqa_sgu_macaron_dit'


def rms_norm(x, g):
    xf = x.astype(jnp.float32)
    y = xf * lax.rsqrt(jnp.mean(xf * xf, axis=-1, keepdims=True) + EPS)
    return (y * g.astype(jnp.float32)).astype(x.dtype)


def modulate(x, g, shift, scale):
    return rms_norm(x, g) * (1 + scale) + shift


def adaln(cond, w, b):
    mod = jax.nn.silu(cond) @ w + b
    return jnp.split(mod[..., None, :], N_MOD, axis=-1)


def swiglu(x, w13, w2):
    a, b = jnp.split(x @ w13, 2, axis=-1)
    return (jax.nn.silu(a) * b) @ w2


def split_cols(p):
    return (p[..., OFF_Q:OFF_K], p[..., OFF_K:OFF_V], p[..., OFF_V:OFF_POOL],
            p[..., OFF_POOL:OFF_SGU], p[..., OFF_SGU:OFF_GATE], p[..., OFF_GATE:IN_COLS])


def axial_rope_tables(L):
    rows = L // GRID_W
    row = jnp.broadcast_to(jnp.arange(rows, dtype=jnp.int32)[:, None], (rows, GRID_W)).reshape(-1)
    col = jnp.broadcast_to(jnp.arange(GRID_W, dtype=jnp.int32)[None, :], (rows, GRID_W)).reshape(-1)
    half = HEAD_DIM // 2
    inv_freq = ROPE_THETA ** (-jnp.arange(0, half, 2, dtype=jnp.float32) / half)
    ang = jnp.stack([row.astype(jnp.float32)[:, None] * inv_freq[None, :],
                     col.astype(jnp.float32)[:, None] * inv_freq[None, :]])
    return jnp.cos(ang), jnp.sin(ang)


def rope_1d(x, cos, sin):
    cos = cos.astype(x.dtype)[None, :, None, :]
    sin = sin.astype(x.dtype)[None, :, None, :]
    x1, x2 = jnp.split(x, 2, axis=-1)
    return jnp.concatenate([x1 * cos - x2 * sin, x2 * cos + x1 * sin], axis=-1)


def apply_axial_rope(x, cos, sin):
    xr, xc = jnp.split(x, 2, axis=-1)
    return jnp.concatenate([rope_1d(xr, cos[0], sin[0]), rope_1d(xc, cos[1], sin[1])], axis=-1)


def gqa_softmax(q, k, v):
    s = jnp.einsum('bqkgd,bskd->bkgqs', q, k).astype(jnp.float32) * (HEAD_DIM ** -0.5)
    p = jax.nn.softmax(s, axis=-1).astype(v.dtype)
    return jnp.einsum('bkgqs,bskd->bqkgd', p, v)


def latent_attention(q, k_lat, v_lat, k_ctx, v_ctx):
    B, L = q.shape[0], q.shape[1]
    G = N_HEADS // N_KV_HEADS
    k_all = jnp.concatenate([k_lat, k_ctx], axis=1)
    v_all = jnp.concatenate([v_lat, v_ctx], axis=1)
    nblk = L // Q_BLOCK
    qb = q.reshape(B, nblk, Q_BLOCK, N_KV_HEADS, G, HEAD_DIM).transpose(1, 0, 2, 3, 4, 5)
    out = lax.map(lambda qblk: gqa_softmax(qblk, k_all, v_all), qb)
    return out.transpose(1, 0, 2, 3, 4, 5).reshape(B, L, ATTN_WIDTH)


def context_attention(q, k, v):
    B, C = q.shape[0], q.shape[1]
    G = N_HEADS // N_KV_HEADS
    out = gqa_softmax(q.reshape(B, C, N_KV_HEADS, G, HEAD_DIM), k, v)
    return out.reshape(B, C, ATTN_WIDTH)


def pool_mix(x, w, scale):
    B, L = x.shape[0], x.shape[1]
    xg = x.reshape(B, L, POOL_GROUPS, POOL_GROUP_DIM)
    xf = xg.astype(jnp.float32)
    cs = jnp.concatenate([jnp.zeros((B, 1, POOL_GROUPS, POOL_GROUP_DIM), jnp.float32),
                          jnp.cumsum(xf, axis=1)], axis=1)
    t = jnp.arange(L, dtype=jnp.int32)
    means = []
    for gi, win in enumerate(POOL_WINDOWS):
        lo = jnp.clip(t - win // 2, 0, L)
        hi = jnp.clip(t + win // 2, 0, L)
        cnt = (hi - lo).astype(jnp.float32)[None, :, None]
        means.append((cs[:, hi, gi] - cs[:, lo, gi]) / cnt)
    pooled = (jnp.stack(means, axis=2) - xf).astype(x.dtype)
    y = jnp.einsum('blgc,gcd->blgd', pooled, w).reshape(B, L, POOL_WIDTH)
    return y * scale


def sgu_mix(uv, norm_g, w_s, b_s):
    B, L = uv.shape[0], uv.shape[1]
    u, v = jnp.split(jax.nn.gelu(uv), 2, axis=-1)
    v = rms_norm(v, norm_g)
    vc = v.reshape(B, L // CHUNK, CHUNK, SGU_GROUPS, SGU_GROUP_DIM)
    s = jnp.einsum('gts,bnsgc->bntgc', w_s, vc) + b_s.T[None, None, :, :, None]
    return u * s.reshape(B, L, SGU_WIDTH)


def merge_branches(pool_out, attn_out, sgu_out, gate_logits, w_br_pool, w_br_attn, w_br_sgu, w_out):
    g = jax.nn.sigmoid(gate_logits)
    merged = (g[..., :D_MODEL] * (pool_out @ w_br_pool)
              + g[..., D_MODEL:2 * D_MODEL] * (attn_out @ w_br_attn)
              + g[..., 2 * D_MODEL:] * (sgu_out @ w_br_sgu))
    return merged @ w_out


def setup_inputs(seed: int = 0) -> dict:
    key = jax.random.key(seed)
    ks = jax.random.split(key, 26)
    f32 = jnp.float32

    def nrm(k, shape, scale):
        return jax.random.normal(k, shape, f32) * scale

    def gain(k, shape):
        return 1.0 + 0.05 * jax.random.normal(k, shape, f32)

    D = D_MODEL
    return {
        'x': nrm(ks[0], (BATCH, SEQ, D), 1.0),
        'c': nrm(ks[1], (BATCH, D), 1.0),
        'ctx': nrm(ks[2], (BATCH, CTX_LEN, D), 1.0),
        'c_ctx': nrm(ks[3], (D,), 1.0),
        'w_ada': nrm(ks[4], (DEPTH, D, N_MOD * D), 0.5 * D ** -0.5),
        'b_ada': nrm(ks[5], (DEPTH, N_MOD * D), 0.02),
        'norm_ffn1': gain(ks[6], (DEPTH, D)),
        'ffn1_w13': nrm(ks[7], (DEPTH, D, 2 * D_FF), D ** -0.5),
        'ffn1_w2': nrm(ks[8], (DEPTH, D_FF, D), D_FF ** -0.5),
        'norm_mix': gain(ks[9], (DEPTH, D)),
        'w_in': nrm(ks[10], (DEPTH, D, IN_COLS), D ** -0.5),
        'q_norm': gain(ks[11], (DEPTH, HEAD_DIM)),
        'k_norm': gain(ks[12], (DEPTH, HEAD_DIM)),
        'pool_w': nrm(ks[13], (DEPTH, POOL_GROUPS, POOL_GROUP_DIM, POOL_GROUP_DIM), POOL_GROUP_DIM ** -0.5),
        'pool_scale': gain(ks[14], (DEPTH, POOL_WIDTH)),
        'sgu_norm': gain(ks[15], (DEPTH, SGU_WIDTH)),
        'sgu_w': nrm(ks[16], (DEPTH, SGU_GROUPS, CHUNK, CHUNK), CHUNK ** -0.5),
        'sgu_b': 1.0 + nrm(ks[17], (DEPTH, SGU_GROUPS, CHUNK), 0.02),
        'w_br_pool': nrm(ks[18], (DEPTH, POOL_WIDTH, D), POOL_WIDTH ** -0.5),
        'w_br_attn': nrm(ks[19], (DEPTH, ATTN_WIDTH, D), ATTN_WIDTH ** -0.5),
        'w_br_sgu': nrm(ks[20], (DEPTH, SGU_WIDTH, D), SGU_WIDTH ** -0.5),
        'w_out': nrm(ks[21], (DEPTH, D, D), D ** -0.5),
        'norm_ffn2': gain(ks[22], (DEPTH, D)),
        'ffn2_w13': nrm(ks[23], (DEPTH, D, 2 * D_FF), D ** -0.5),
        'ffn2_w2': nrm(ks[24], (DEPTH, D_FF, D), D_FF ** -0.5),
        'final_norm': gain(ks[25], (D,)),
    }


def reference(x, c, ctx, c_ctx, w_ada, b_ada, norm_ffn1, ffn1_w13, ffn1_w2, norm_mix, w_in,
              q_norm, k_norm, pool_w, pool_scale, sgu_norm, sgu_w, sgu_b, w_br_pool, w_br_attn,
              w_br_sgu, w_out, norm_ffn2, ffn2_w13, ffn2_w2, final_norm):
    B, L = x.shape[0], x.shape[1]
    C = ctx.shape[1]
    cos, sin = axial_rope_tables(L)
    h, hc = x, ctx
    for l in range(DEPTH):
        last = l == DEPTH - 1
        m = adaln(c, w_ada[l], b_ada[l])
        mc = adaln(c_ctx, w_ada[l], b_ada[l])

        h = h + 0.5 * m[2] * swiglu(modulate(h, norm_ffn1[l], m[0], m[1]), ffn1_w13[l], ffn1_w2[l])
        hc = hc + 0.5 * mc[2] * swiglu(modulate(hc, norm_ffn1[l], mc[0], mc[1]), ffn1_w13[l], ffn1_w2[l])

        hn = modulate(h, norm_mix[l], m[3], m[4])
        hnc = modulate(hc, norm_mix[l], mc[3], mc[4])
        q, k, v, pool_in, sgu_in, gate_logits = split_cols(hn @ w_in[l])
        q = apply_axial_rope(rms_norm(q.reshape(B, L, N_HEADS, HEAD_DIM), q_norm[l]), cos, sin)
        k = apply_axial_rope(rms_norm(k.reshape(B, L, N_KV_HEADS, HEAD_DIM), k_norm[l]), cos, sin)
        v = v.reshape(B, L, N_KV_HEADS, HEAD_DIM)
        if last:
            kv_c = hnc @ w_in[l][:, OFF_K:OFF_POOL]
            k_c, v_c = kv_c[..., :KV_WIDTH], kv_c[..., KV_WIDTH:]
        else:
            q_c, k_c, v_c, pool_c, sgu_c, gate_c = split_cols(hnc @ w_in[l])
        k_c = rms_norm(k_c.reshape(B, C, N_KV_HEADS, HEAD_DIM), k_norm[l])
        v_c = v_c.reshape(B, C, N_KV_HEADS, HEAD_DIM)

        attn = latent_attention(q, k, v, k_c, v_c)
        out = merge_branches(pool_mix(pool_in, pool_w[l], pool_scale[l]), attn,
                             sgu_mix(sgu_in, sgu_norm[l], sgu_w[l], sgu_b[l]), gate_logits,
                             w_br_pool[l], w_br_attn[l], w_br_sgu[l], w_out[l])
        h = h + m[5] * out

        if not last:
            q_c = rms_norm(q_c.reshape(B, C, N_HEADS, HEAD_DIM), q_norm[l])
            attn_c = context_attention(q_c, k_c, v_c)
            out_c = merge_branches(pool_mix(pool_c, pool_w[l], pool_scale[l]), attn_c,
                                   sgu_mix(sgu_c, sgu_norm[l], sgu_w[l], sgu_b[l]), gate_c,
                                   w_br_pool[l], w_br_attn[l], w_br_sgu[l], w_out[l])
            hc = hc + mc[5] * out_c
            hc = hc + 0.5 * mc[8] * swiglu(modulate(hc, norm_ffn2[l], mc[6], mc[7]), ffn2_w13[l], ffn2_w2[l])

        h = h + 0.5 * m[8] * swiglu(modulate(h, norm_ffn2[l], m[6], m[7]), ffn2_w13[l], ffn2_w2[l])
    return rms_norm(h, final_norm)
```

```python
import functools

import jax
import jax.numpy as jnp
from jax import lax
from jax.experimental import pallas as pl
from jax.experimental.pallas import tpu as pltpu

D_MODEL = 1024
BATCH = 16
SEQ = 2048
DEPTH = 2
GRID_W = 64
CTX_LEN = 256
N_HEADS = 8
N_KV_HEADS = 2
HEAD_DIM = 64
ATTN_WIDTH = N_HEADS * HEAD_DIM
KV_WIDTH = N_KV_HEADS * HEAD_DIM
ROPE_THETA = 10000.0
POOL_WINDOWS = (2, 4, 8, 16)
POOL_GROUPS = 4
POOL_GROUP_DIM = 64
POOL_WIDTH = POOL_GROUPS * POOL_GROUP_DIM
SGU_GROUPS = 4
SGU_GROUP_DIM = 64
SGU_WIDTH = SGU_GROUPS * SGU_GROUP_DIM
CHUNK = 128
D_FF = 2816
EPS = 1e-6
N_MOD = 9
OFF_GATE = ATTN_WIDTH + 2 * KV_WIDTH + POOL_WIDTH + 2 * SGU_WIDTH
IN_COLS = OFF_GATE + 3 * D_MODEL

TOKENS = SEQ + CTX_LEN
GROUP_HEADS = N_HEADS // N_KV_HEADS
GROUP_WIDTH = GROUP_HEADS * HEAD_DIM
LANES = 128
SUBLANES = 8
TM = 256
N_LAT = SEQ // TM
N_ALL = TOKENS // TM
COND_ROWS = 24
POOL_HALO = max(POOL_WINDOWS) // 2
VMEM_LIMIT = 56 * 1024 * 1024

F32 = jnp.float32
BF16 = jnp.bfloat16


def _dot(a, b):
    return jnp.dot(a, b, preferred_element_type=F32)


def _rms(x, g):
    return x * lax.rsqrt(jnp.mean(x * x, axis=-1, keepdims=True) + EPS) * g


def _modulate(x, g, mod_ref, base):
    shift = mod_ref[base:base + 1, :]
    scale = mod_ref[base + 1:base + 2, :]
    return _rms(x, g) * (1.0 + scale) + shift


def _const_spec(shape):
    zeros = (0,) * len(shape)
    return pl.BlockSpec(shape, lambda *_: zeros, pipeline_mode=pl.Buffered(1))


def _tile_spec(width):
    return pl.BlockSpec((None, TM, width), lambda b, i: (b, i, 0))


def _mod_spec():
    return pl.BlockSpec((None, N_MOD, D_MODEL),
                        lambda b, i: (jnp.where(i >= N_LAT, BATCH, b), 0, 0))


def _params():
    return pltpu.CompilerParams(vmem_limit_bytes=VMEM_LIMIT)


def _adaln_kernel(cond_ref, w_ref, b_ref, o_ref):
    cond = cond_ref[...]
    s = (cond * jax.nn.sigmoid(cond)).astype(BF16)
    o_ref[...] = _dot(s, w_ref[...].astype(BF16)) + b_ref[...]


def _adaln(cond, w_ada, b_ada):
    return pl.pallas_call(
        _adaln_kernel,
        grid=(DEPTH, N_MOD),
        in_specs=[
            pl.BlockSpec((COND_ROWS, D_MODEL), lambda l, j: (0, 0)),
            pl.BlockSpec((None, D_MODEL, D_MODEL), lambda l, j: (l, 0, j)),
            pl.BlockSpec((None, 1, D_MODEL), lambda l, j: (l, 0, j)),
        ],
        out_specs=pl.BlockSpec((None, COND_ROWS, D_MODEL), lambda l, j: (l, 0, j)),
        out_shape=jax.ShapeDtypeStruct((DEPTH, COND_ROWS, N_MOD * D_MODEL), F32),
        compiler_params=_params(),
        name="adaln",
    )(cond, w_ada, b_ada.reshape(DEPTH, 1, N_MOD * D_MODEL))


def _ffn_kernel(h_ref, mod_ref, g_ref, w13_ref, w2_ref, fg_ref, o_ref, *, mod_base, final):
    x = h_ref[...]
    xn = _modulate(x, g_ref[...], mod_ref, mod_base).astype(BF16)
    a = _dot(xn, w13_ref[:, :D_FF])
    b = _dot(xn, w13_ref[:, D_FF:])
    u = (a * jax.nn.sigmoid(a) * b).astype(BF16)
    y = _dot(u, w2_ref[...])
    out = x + (0.5 * mod_ref[mod_base + 2:mod_base + 3, :]) * y
    if final:
        out = _rms(out, fg_ref[...])
    o_ref[...] = out


def _ffn(h, mod, g, w13, w2, fg, *, mod_base, n_tiles, final):
    rows = n_tiles * TM
    return pl.pallas_call(
        functools.partial(_ffn_kernel, mod_base=mod_base, final=final),
        grid=(BATCH, n_tiles),
        in_specs=[
            _tile_spec(D_MODEL),
            _mod_spec(),
            _const_spec((1, D_MODEL)),
            _const_spec((D_MODEL, 2 * D_FF)),
            _const_spec((D_FF, D_MODEL)),
            _const_spec((1, D_MODEL)),
        ],
        out_specs=_tile_spec(D_MODEL),
        out_shape=jax.ShapeDtypeStruct((BATCH, rows, D_MODEL), F32),
        compiler_params=_params(),
        name="ffn",
    )(h, mod, g, w13, w2, fg)


def _head_norm_rope(x, gmat, g, cos, sin, first_half):
    ms = _dot((x * x).astype(BF16), gmat)
    y = x * lax.rsqrt(ms + EPS) * g
    up = pltpu.roll(y, LANES - HEAD_DIM // 4, 1)
    dn = pltpu.roll(y, HEAD_DIM // 4, 1)
    return y * cos + jnp.where(first_half, up, dn) * sin


def _replicate_kv_heads(x, low_half):
    sw = pltpu.roll(x, HEAD_DIM, 1)
    h0 = jnp.where(low_half, x, sw)
    h1 = jnp.where(low_half, sw, x)
    return jnp.concatenate([h0, h0], axis=1), jnp.concatenate([h1, h1], axis=1)


def _proj_kernel(h_ref, mod_ref, g_ref, w_ref, qg_ref, kg_ref, cos_ref, sin_ref, gmat_ref,
                 q_ref, k_ref, v_ref, pool_ref, sgu_ref):
    hn = _modulate(h_ref[...], g_ref[...], mod_ref, 3).astype(BF16)
    p = _dot(hn, w_ref[...])
    lane = lax.broadcasted_iota(jnp.int32, (1, LANES), 1)
    first_half = (lane % (HEAD_DIM // 2)) < (HEAD_DIM // 4)
    low_half = lane < HEAD_DIM
    gmat, cos, sin = gmat_ref[...], cos_ref[...], sin_ref[...]
    for c in range(ATTN_WIDTH // LANES):
        qc = _head_norm_rope(p[:, c * LANES:(c + 1) * LANES], gmat, qg_ref[...], cos, sin, first_half)
        q_ref[:, c * LANES:(c + 1) * LANES] = (qc * (HEAD_DIM ** -0.5)).astype(BF16)
    k = _head_norm_rope(p[:, ATTN_WIDTH:ATTN_WIDTH + KV_WIDTH], gmat, kg_ref[...], cos, sin, first_half)
    k0, k1 = _replicate_kv_heads(k, low_half)
    k_ref[0] = k0.astype(BF16)
    k_ref[1] = k1.astype(BF16)
    v0, v1 = _replicate_kv_heads(p[:, ATTN_WIDTH + KV_WIDTH:ATTN_WIDTH + 2 * KV_WIDTH], low_half)
    v_ref[0] = v0.astype(BF16)
    v_ref[1] = v1.astype(BF16)
    off = ATTN_WIDTH + 2 * KV_WIDTH
    pool_ref[...] = p[:, off:off + POOL_WIDTH]
    sgu_ref[...] = p[:, off + POOL_WIDTH:off + POOL_WIDTH + 2 * SGU_WIDTH]


def _proj(h, mod, g, w, qg, kg, cos, sin, gmat):
    kv_spec = pl.BlockSpec((None, N_KV_HEADS, TM, GROUP_WIDTH), lambda b, i: (b, 0, i, 0))
    tab_spec = pl.BlockSpec((TM, LANES), lambda b, i: (i, 0))
    return pl.pallas_call(
        _proj_kernel,
        grid=(BATCH, N_ALL),
        in_specs=[
            _tile_spec(D_MODEL),
            _mod_spec(),
            _const_spec((1, D_MODEL)),
            _const_spec((D_MODEL, OFF_GATE)),
            _const_spec((1, LANES)),
            _const_spec((1, LANES)),
            tab_spec,
            tab_spec,
            _const_spec((LANES, LANES)),
        ],
        out_specs=[
            _tile_spec(ATTN_WIDTH),
            kv_spec,
            kv_spec,
            _tile_spec(POOL_WIDTH),
            _tile_spec(2 * SGU_WIDTH),
        ],
        out_shape=[
            jax.ShapeDtypeStruct((BATCH, TOKENS, ATTN_WIDTH), BF16),
            jax.ShapeDtypeStruct((BATCH, N_KV_HEADS, TOKENS, GROUP_WIDTH), BF16),
            jax.ShapeDtypeStruct((BATCH, N_KV_HEADS, TOKENS, GROUP_WIDTH), BF16),
            jax.ShapeDtypeStruct((BATCH, TOKENS, POOL_WIDTH), F32),
            jax.ShapeDtypeStruct((BATCH, TOKENS, 2 * SGU_WIDTH), F32),
        ],
        compiler_params=_params(),
        name="proj",
    )(h, mod, g, w, qg, kg, cos, sin, gmat)


def _attend(q, k, v, o_ref):
    lane = lax.broadcasted_iota(jnp.int32, (1, GROUP_WIDTH), 1)
    acc = jnp.zeros((TM, GROUP_WIDTH), F32)
    for j in range(GROUP_HEADS):
        mask = (lane >= j * HEAD_DIM) & (lane < (j + 1) * HEAD_DIM)
        qj = jnp.where(mask, q, jnp.zeros_like(q))
        s = lax.dot_general(qj, k, (((1,), (1,)), ((), ())), preferred_element_type=F32)
        p = jnp.exp(s - jnp.max(s, axis=-1, keepdims=True))
        denom = jnp.sum(p, axis=-1, keepdims=True)
        pv = _dot(p.astype(BF16), v)
        acc = acc + jnp.where(mask, pv / denom, 0.0)
    o_ref[...] = acc.astype(BF16)


def _attn_kernel(q_ref, k_ref, v_ref, o_ref):
    i = pl.program_id(2)

    @pl.when(i < N_LAT)
    def _():
        _attend(q_ref[...], k_ref[...], v_ref[...], o_ref)

    @pl.when(i >= N_LAT)
    def _():
        _attend(q_ref[...], k_ref[SEQ:, :], v_ref[SEQ:, :], o_ref)


def _attention(q, k, v, *, n_tiles):
    kv_spec = pl.BlockSpec((None, None, TOKENS, GROUP_WIDTH), lambda b, h, i: (b, h, 0, 0))
    q_spec = pl.BlockSpec((None, TM, GROUP_WIDTH), lambda b, h, i: (b, i, h))
    return pl.pallas_call(
        _attn_kernel,
        grid=(BATCH, N_KV_HEADS, n_tiles),
        in_specs=[q_spec, kv_spec, kv_spec],
        out_specs=q_spec,
        out_shape=jax.ShapeDtypeStruct((BATCH, n_tiles * TM, ATTN_WIDTH), BF16),
        compiler_params=_params(),
        name="attention",
    )(q, k, v)


def _pool_branch(pool_ref, prev_ref, next_ref, xe_ref, q2_ref, q4_ref, q8_ref, wbd_ref, scale_ref, i, n_tiles):
    first = (i == 0) | (i == N_LAT)
    last = (i == N_LAT - 1) | (i == n_tiles - 1)
    x = pool_ref[...]
    xe_ref[0:POOL_HALO, :] = jnp.where(first, 0.0, prev_ref[...])
    xe_ref[POOL_HALO:POOL_HALO + TM, :] = x
    xe_ref[POOL_HALO + TM:2 * POOL_HALO + TM, :] = jnp.where(last, 0.0, next_ref[...])
    xe_ref[2 * POOL_HALO + TM:, :] = jnp.zeros((2 * POOL_HALO, POOL_WIDTH), F32)
    q2_ref[...] = xe_ref[0:TM + 24, :] + xe_ref[1:TM + 25, :]
    q4_ref[...] = q2_ref[0:TM + 16, :] + q2_ref[2:TM + 18, :]
    q8_ref[...] = q4_ref[0:TM + 8, LANES:] + q4_ref[4:TM + 12, LANES:]
    q16 = q8_ref[0:TM, :] + q8_ref[8:TM + 8, :]
    lane = lax.broadcasted_iota(jnp.int32, (1, LANES), 1)
    low = lane < POOL_GROUP_DIM
    sums = jnp.concatenate([
        jnp.where(low, q2_ref[7:7 + TM, 0:LANES], q4_ref[6:6 + TM, 0:LANES]),
        jnp.where(low, q8_ref[4:4 + TM, :], q16),
    ], axis=1)
    lane_w = lax.broadcasted_iota(jnp.int32, (1, POOL_WIDTH), 1)
    half = jnp.where(lane_w < 64, 1, jnp.where(lane_w < 128, 2, jnp.where(lane_w < 192, 4, 8)))
    is_ctx = i >= N_LAT
    seg_len = jnp.where(is_ctx, CTX_LEN, SEQ)
    pos = (i - jnp.where(is_ctx, N_LAT, 0)) * TM + lax.broadcasted_iota(jnp.int32, (TM, 1), 0)
    cnt = jnp.minimum(pos + half, seg_len) - jnp.maximum(pos - half, 0)
    pooled = sums / cnt.astype(F32) - x
    return _dot(pooled.astype(BF16), wbd_ref[...]) * scale_ref[...]


def _gelu_tanh(x):
    return x * (0.5 * (1.0 + jnp.tanh(0.7978845608028654 * (x + 0.044715 * (x * x * x)))))


def _sgu_branch(sgu_ref, norm_ref, wcat_ref, bias_ref):
    gl = _gelu_tanh(sgu_ref[...])
    u = gl[:, :SGU_WIDTH]
    v = _rms(gl[:, SGU_WIDTH:], norm_ref[...])
    lane = lax.broadcasted_iota(jnp.int32, (1, SGU_WIDTH), 1)
    outs = []
    for c in range(TM // CHUNK):
        vc = v[c * CHUNK:(c + 1) * CHUNK, :]
        stack = jnp.concatenate(
            [jnp.where((lane >= g * SGU_GROUP_DIM) & (lane < (g + 1) * SGU_GROUP_DIM), vc, 0.0)
             for g in range(SGU_GROUPS)], axis=0).astype(BF16)
        outs.append(_dot(wcat_ref[...], stack) + bias_ref[...])
    return u * jnp.concatenate(outs, axis=0)


def _merge_kernel(h_ref, mod_ref, g_ref, attn_ref, pool_ref, prev_ref, next_ref, sgu_ref,
                  wgate_ref, wbd_ref, pscale_ref, snorm_ref, wcat_ref, sbias_ref,
                  wbp_ref, wba_ref, wbs_ref, wout_ref, o_ref,
                  xe_ref, q2_ref, q4_ref, q8_ref, *, n_tiles):
    i = pl.program_id(1)
    x = h_ref[...]
    hn = _modulate(x, g_ref[...], mod_ref, 3).astype(BF16)
    pool_out = _pool_branch(pool_ref, prev_ref, next_ref, xe_ref, q2_ref, q4_ref, q8_ref,
                            wbd_ref, pscale_ref, i, n_tiles)
    sgu_out = _sgu_branch(sgu_ref, snorm_ref, wcat_ref, sbias_ref)
    merged = jax.nn.sigmoid(_dot(hn, wgate_ref[:, :D_MODEL])) * _dot(pool_out.astype(BF16), wbp_ref[...])
    merged += jax.nn.sigmoid(_dot(hn, wgate_ref[:, D_MODEL:2 * D_MODEL])) * _dot(attn_ref[...], wba_ref[...])
    merged += jax.nn.sigmoid(_dot(hn, wgate_ref[:, 2 * D_MODEL:])) * _dot(sgu_out.astype(BF16), wbs_ref[...])
    o_ref[...] = x + mod_ref[5:6, :] * _dot(merged.astype(BF16), wout_ref[...])


def _merge(h, mod, g, attn, pool_in, sgu_in, wgate, wbd, pscale, snorm, wcat, sbias, wbp, wba, wbs, wout,
           *, n_tiles):
    blocks_per_tile = TM // POOL_HALO
    last_block = TOKENS // POOL_HALO - 1
    prev_spec = pl.BlockSpec((None, POOL_HALO, POOL_WIDTH),
                             lambda b, i: (b, jnp.maximum(i * blocks_per_tile - 1, 0), 0))
    next_spec = pl.BlockSpec((None, POOL_HALO, POOL_WIDTH),
                             lambda b, i: (b, jnp.minimum((i + 1) * blocks_per_tile, last_block), 0))
    return pl.pallas_call(
        functools.partial(_merge_kernel, n_tiles=n_tiles),
        grid=(BATCH, n_tiles),
        in_specs=[
            _tile_spec(D_MODEL),
            _mod_spec(),
            _const_spec((1, D_MODEL)),
            _tile_spec(ATTN_WIDTH),
            _tile_spec(POOL_WIDTH),
            prev_spec,
            next_spec,
            _tile_spec(2 * SGU_WIDTH),
            _const_spec((D_MODEL, 3 * D_MODEL)),
            _const_spec((POOL_WIDTH, POOL_WIDTH)),
            _const_spec((1, POOL_WIDTH)),
            _const_spec((1, SGU_WIDTH)),
            _const_spec((CHUNK, SGU_GROUPS * CHUNK)),
            _const_spec((CHUNK, SGU_WIDTH)),
            _const_spec((POOL_WIDTH, D_MODEL)),
            _const_spec((ATTN_WIDTH, D_MODEL)),
            _const_spec((SGU_WIDTH, D_MODEL)),
            _const_spec((D_MODEL, D_MODEL)),
        ],
        out_specs=_tile_spec(D_MODEL),
        out_shape=jax.ShapeDtypeStruct((BATCH, n_tiles * TM, D_MODEL), F32),
        scratch_shapes=[
            pltpu.VMEM((TM + 32, POOL_WIDTH), F32),
            pltpu.VMEM((TM + 24, POOL_WIDTH), F32),
            pltpu.VMEM((TM + 16, POOL_WIDTH), F32),
            pltpu.VMEM((TM + 8, LANES), F32),
        ],
        compiler_params=_params(),
        name="merge",
    )(h, mod, g, attn, pool_in, pool_in, pool_in, sgu_in, wgate, wbd, pscale, snorm, wcat, sbias,
      wbp, wba, wbs, wout)


def _rope_tables():
    pos = jnp.arange(SEQ, dtype=jnp.int32)
    row = (pos // GRID_W).astype(F32)[:, None]
    col = (pos % GRID_W).astype(F32)[:, None]
    half = HEAD_DIM // 2
    inv_freq = ROPE_THETA ** (-jnp.arange(0, half, 2, dtype=F32) / half)[None, :]
    ang = jnp.concatenate([row * inv_freq, row * inv_freq, col * inv_freq, col * inv_freq], axis=1)
    sign = jnp.tile(jnp.concatenate([-jnp.ones((half // 2,), F32), jnp.ones((half // 2,), F32)]), 2)[None, :]
    cos = jnp.concatenate([jnp.cos(ang), jnp.ones((CTX_LEN, HEAD_DIM), F32)], axis=0)
    sin = jnp.concatenate([jnp.sin(ang) * sign, jnp.zeros((CTX_LEN, HEAD_DIM), F32)], axis=0)
    return jnp.tile(cos, (1, LANES // HEAD_DIM)), jnp.tile(sin, (1, LANES // HEAD_DIM))


def _block_diag(blocks):
    n, r, c = blocks.shape
    eye = jnp.eye(n, dtype=blocks.dtype)
    return (eye[:, None, :, None] * blocks[:, :, None, :]).reshape(n * r, n * c)


def kernel(x, c, ctx, c_ctx, w_ada, b_ada, norm_ffn1, ffn1_w13, ffn1_w2, norm_mix, w_in, q_norm, k_norm,
           pool_w, pool_scale, sgu_norm, sgu_w, sgu_b, w_br_pool, w_br_attn, w_br_sgu, w_out, norm_ffn2,
           ffn2_w13, ffn2_w2, final_norm):
    cond = jnp.concatenate([c, c_ctx[None, :], jnp.zeros((COND_ROWS - BATCH - 1, D_MODEL), F32)], axis=0)
    mods = _adaln(cond, w_ada, b_ada).reshape(DEPTH, COND_ROWS, N_MOD, D_MODEL)
    cos, sin = _rope_tables()
    gmat = _block_diag(jnp.full((LANES // HEAD_DIM, HEAD_DIM, HEAD_DIM), 1.0 / HEAD_DIM, F32)).astype(BF16)
    fg = final_norm.reshape(1, D_MODEL)

    h = jnp.concatenate([x, ctx], axis=1)
    for l in range(DEPTH):
        last = l == DEPTH - 1
        mod = mods[l]
        n_tiles = N_LAT if last else N_ALL
        w_in_l = w_in[l].astype(BF16)
        h = _ffn(h, mod, norm_ffn1[l].reshape(1, D_MODEL), ffn1_w13[l].astype(BF16), ffn1_w2[l].astype(BF16),
                 fg, mod_base=0, n_tiles=N_ALL, final=False)
        q, k, v, pool_in, sgu_in = _proj(
            h, mod, norm_mix[l].reshape(1, D_MODEL), w_in_l[:, :OFF_GATE],
            jnp.tile(q_norm[l], LANES // HEAD_DIM).reshape(1, LANES),
            jnp.tile(k_norm[l], LANES // HEAD_DIM).reshape(1, LANES), cos, sin, gmat)
        attn = _attention(q, k, v, n_tiles=n_tiles)
        h = _merge(
            h, mod, norm_mix[l].reshape(1, D_MODEL), attn, pool_in, sgu_in, w_in_l[:, OFF_GATE:],
            _block_diag(pool_w[l]).astype(BF16), pool_scale[l].reshape(1, POOL_WIDTH),
            sgu_norm[l].reshape(1, SGU_WIDTH),
            jnp.transpose(sgu_w[l], (1, 0, 2)).reshape(CHUNK, SGU_GROUPS * CHUNK).astype(BF16),
            jnp.repeat(sgu_b[l].T, SGU_GROUP_DIM, axis=1),
            w_br_pool[l].astype(BF16), w_br_attn[l].astype(BF16), w_br_sgu[l].astype(BF16),
            w_out[l].astype(BF16), n_tiles=n_tiles)
        h = _ffn(h, mod, norm_ffn2[l].reshape(1, D_MODEL), ffn2_w13[l].astype(BF16), ffn2_w2[l].astype(BF16),
                 fg, mod_base=6, n_tiles=n_tiles, final=last)
    return h
```

```python
import functools

import jax
import jax.numpy as jnp
from jax import lax
from jax.experimental import pallas as pl
from jax.experimental.pallas import tpu as pltpu

D_MODEL = 1024
BATCH = 16
SEQ = 2048
DEPTH = 2
GRID_W = 64
CTX_LEN = 256
N_HEADS = 8
N_KV_HEADS = 2
HEAD_DIM = 64
ATTN_WIDTH = N_HEADS * HEAD_DIM
KV_WIDTH = N_KV_HEADS * HEAD_DIM
ROPE_THETA = 10000.0
POOL_WINDOWS = (2, 4, 8, 16)
POOL_GROUPS = 4
POOL_GROUP_DIM = 64
POOL_WIDTH = POOL_GROUPS * POOL_GROUP_DIM
SGU_GROUPS = 4
SGU_GROUP_DIM = 64
SGU_WIDTH = SGU_GROUPS * SGU_GROUP_DIM
CHUNK = 128
D_FF = 2816
EPS = 1e-6
N_MOD = 9
OFF_K = ATTN_WIDTH
OFF_POOL = ATTN_WIDTH + 2 * KV_WIDTH
OFF_GATE = OFF_POOL + POOL_WIDTH + 2 * SGU_WIDTH

LAT_ROWS = BATCH * SEQ
CTX_ROWS = BATCH * CTX_LEN
ALL_ROWS = LAT_ROWS + CTX_ROWS
GROUP_HEADS = N_HEADS // N_KV_HEADS
LANES = 128
COND_ROWS = 24
POOL_HALO = max(POOL_WINDOWS) // 2
BF16_ROWS = 16
V_ROWS = HEAD_DIM + BF16_ROWS
VT_ROWS = N_KV_HEADS * V_ROWS
Q_SCALE = HEAD_DIM ** -0.5 * 1.4426950408889634
TM_FFN = 512
TM_PROJ = 512
TM_MERGE = CTX_LEN
TQ = CTX_LEN
KEY_CHUNK = 256
VMEM_LIMIT = 56 * 1024 * 1024

F32 = jnp.float32
BF16 = jnp.bfloat16


def _dot(a, b):
    return jnp.dot(a, b, preferred_element_type=F32)


def _rms(x, g):
    return x * lax.rsqrt(jnp.mean(x * x, axis=-1, keepdims=True) + EPS) * g


def _modulate(x, g, mod_ref, base):
    shift = mod_ref[base:base + 1, :]
    scale = mod_ref[base + 1:base + 2, :]
    return _rms(x, g) * (1.0 + scale) + shift


def _const_spec(shape):
    zeros = (0,) * len(shape)
    return pl.BlockSpec(shape, lambda *_: zeros, pipeline_mode=pl.Buffered(1))


def _row_spec(tm, width):
    return pl.BlockSpec((tm, width), lambda r: (r, 0))


def _col_spec(height, tm):
    return pl.BlockSpec((height, tm), lambda r: (0, r))


def _mod_spec(tm):
    lat_tiles = LAT_ROWS // tm
    per_batch = SEQ // tm
    return pl.BlockSpec((None, N_MOD, D_MODEL),
                        lambda r: (jnp.where(r >= lat_tiles, BATCH, r // per_batch), 0, 0))


def _params():
    return pltpu.CompilerParams(vmem_limit_bytes=VMEM_LIMIT)


def _adaln_kernel(cond_ref, w_ref, b_ref, o_ref):
    cond = cond_ref[...]
    s = (cond * jax.nn.sigmoid(cond)).astype(BF16)
    o_ref[...] = _dot(s, w_ref[...].astype(BF16)) + b_ref[...]


def _adaln(cond, w_ada, b_ada):
    return pl.pallas_call(
        _adaln_kernel,
        grid=(DEPTH, N_MOD),
        in_specs=[
            pl.BlockSpec((COND_ROWS, D_MODEL), lambda l, j: (0, 0)),
            pl.BlockSpec((None, D_MODEL, D_MODEL), lambda l, j: (l, 0, j)),
            pl.BlockSpec((None, 1, D_MODEL), lambda l, j: (l, 0, j)),
        ],
        out_specs=pl.BlockSpec((None, COND_ROWS, D_MODEL), lambda l, j: (l, 0, j)),
        out_shape=jax.ShapeDtypeStruct((DEPTH, COND_ROWS, N_MOD * D_MODEL), F32),
        compiler_params=_params(),
        name="adaln",
    )(cond, w_ada, b_ada.reshape(DEPTH, 1, N_MOD * D_MODEL))


def _ffn_kernel(*refs, mod_base, final, split_input):
    if split_input:
        lat_ref, ctx_ref, mod_ref, g_ref, w13_ref, w2_ref, fg_ref, o_ref = refs
        x = jnp.where(pl.program_id(0) < LAT_ROWS // TM_FFN, lat_ref[...], ctx_ref[...])
    else:
        h_ref, mod_ref, g_ref, w13_ref, w2_ref, fg_ref, o_ref = refs
        x = h_ref[...]
    xn = _modulate(x, g_ref[...], mod_ref, mod_base).astype(BF16)
    a = _dot(xn, w13_ref[:, :D_FF])
    b = _dot(xn, w13_ref[:, D_FF:])
    u = (a * jax.nn.sigmoid(a) * b).astype(BF16)
    y = _dot(u, w2_ref[...])
    out = x + (0.5 * mod_ref[mod_base + 2:mod_base + 3, :]) * y
    if final:
        out = _rms(out, fg_ref[...])
    o_ref[...] = out


def _ffn(hs, mod, g, w13, w2, fg, *, mod_base, rows, final):
    split_input = len(hs) == 2
    lat_tiles = LAT_ROWS // TM_FFN
    if split_input:
        h_specs = [pl.BlockSpec((TM_FFN, D_MODEL), lambda r: (jnp.minimum(r, lat_tiles - 1), 0)),
                   pl.BlockSpec((TM_FFN, D_MODEL), lambda r: (jnp.maximum(r - lat_tiles, 0), 0))]
    else:
        h_specs = [_row_spec(TM_FFN, D_MODEL)]
    return pl.pallas_call(
        functools.partial(_ffn_kernel, mod_base=mod_base, final=final, split_input=split_input),
        grid=(rows // TM_FFN,),
        in_specs=h_specs + [
            _mod_spec(TM_FFN),
            _const_spec((1, D_MODEL)),
            _const_spec((D_MODEL, 2 * D_FF)),
            _const_spec((D_FF, D_MODEL)),
            _const_spec((1, D_MODEL)),
        ],
        out_specs=_row_spec(TM_FFN, D_MODEL),
        out_shape=jax.ShapeDtypeStruct((rows, D_MODEL), F32),
        compiler_params=_params(),
        name="ffn",
    )(*hs, mod, g, w13, w2, fg)


def _head_norm_rope(x, gmat, g, cos, sin, first_half):
    ms = _dot((x * x).astype(BF16), gmat)
    y = x * lax.rsqrt(ms + EPS) * g
    up = pltpu.roll(y, LANES - HEAD_DIM // 4, 1)
    dn = pltpu.roll(y, HEAD_DIM // 4, 1)
    return y * cos + jnp.where(first_half, up, dn) * sin


def _proj_kernel(*refs, kv_only):
    if kv_only:
        h_ref, mod_ref, g_ref, w_ref, kg_ref, cos_ref, sin_ref, gmat_ref, k_ref, vt_ref = refs
    else:
        (h_ref, mod_ref, g_ref, w_ref, qg_ref, kg_ref, cos_ref, sin_ref, gmat_ref,
         qt_ref, k_ref, vt_ref, pool_ref, sgu_ref) = refs
    hn = _modulate(h_ref[...], g_ref[...], mod_ref, 3).astype(BF16)
    p = _dot(hn, w_ref[...])
    lane = lax.broadcasted_iota(jnp.int32, (1, LANES), 1)
    first_half = (lane % (HEAD_DIM // 2)) < (HEAD_DIM // 4)
    gmat, cos, sin = gmat_ref[...], cos_ref[...], sin_ref[...]
    base = 0 if kv_only else OFF_K
    if not kv_only:
        for c in range(ATTN_WIDTH // LANES):
            qc = _head_norm_rope(p[:, c * LANES:(c + 1) * LANES], gmat, qg_ref[...], cos, sin, first_half)
            qt_ref[c * LANES:(c + 1) * LANES, :] = (qc * Q_SCALE).T.astype(BF16)
        pool_ref[...] = p[:, OFF_POOL:OFF_POOL + POOL_WIDTH]
        sgu_ref[...] = p[:, OFF_POOL + POOL_WIDTH:OFF_GATE]
    k = _head_norm_rope(p[:, base:base + KV_WIDTH], gmat, kg_ref[...], cos, sin, first_half)
    k_ref[...] = k.astype(BF16)
    vt = p[:, base + KV_WIDTH:base + 2 * KV_WIDTH].T.astype(BF16)
    ones = jnp.ones((V_ROWS - HEAD_DIM, vt.shape[1]), BF16)
    for g in range(N_KV_HEADS):
        vt_ref[g * V_ROWS:g * V_ROWS + HEAD_DIM, :] = vt[g * HEAD_DIM:(g + 1) * HEAD_DIM, :]
        vt_ref[g * V_ROWS + HEAD_DIM:(g + 1) * V_ROWS, :] = ones


def _proj(h, mod, g, w, qg, kg, cos, sin, gmat, *, first_tile, n_tiles, kv_only):
    tm = TM_PROJ
    lat_tiles = LAT_ROWS // tm
    per_batch = SEQ // tm
    rows = n_tiles * tm
    h_spec = pl.BlockSpec((tm, D_MODEL), lambda r: (r + first_tile, 0))
    mod_spec = pl.BlockSpec(
        (None, N_MOD, D_MODEL),
        lambda r: (jnp.where(r + first_tile >= lat_tiles, BATCH, (r + first_tile) // per_batch), 0, 0))
    tab_spec = pl.BlockSpec(
        (tm, LANES), lambda r: (jnp.where(r + first_tile >= lat_tiles, per_batch, (r + first_tile) % per_batch), 0))
    gains = [_const_spec((1, LANES))] * (1 if kv_only else 2)
    in_specs = [h_spec, mod_spec, _const_spec((1, D_MODEL)), _const_spec(w.shape)] + gains + [
        tab_spec, tab_spec, _const_spec((LANES, LANES))]
    out_specs = [_row_spec(tm, KV_WIDTH), _col_spec(VT_ROWS, tm)]
    out_shape = [jax.ShapeDtypeStruct((rows, KV_WIDTH), BF16), jax.ShapeDtypeStruct((VT_ROWS, rows), BF16)]
    if not kv_only:
        out_specs = [_col_spec(ATTN_WIDTH, tm)] + out_specs + [_row_spec(tm, POOL_WIDTH), _row_spec(tm, 2 * SGU_WIDTH)]
        out_shape = ([jax.ShapeDtypeStruct((ATTN_WIDTH, rows), BF16)] + out_shape
                     + [jax.ShapeDtypeStruct((rows, POOL_WIDTH), F32),
                        jax.ShapeDtypeStruct((rows, 2 * SGU_WIDTH), F32)])
    args = (h, mod, g, w) + ((kg,) if kv_only else (qg, kg)) + (cos, sin, gmat)
    return pl.pallas_call(
        functools.partial(_proj_kernel, kv_only=kv_only),
        grid=(n_tiles,),
        in_specs=in_specs,
        out_specs=out_specs,
        out_shape=out_shape,
        compiler_params=_params(),
        name="proj_kv" if kv_only else "proj",
    )(*args)


def _attend(qt_ref, k_refs, vt_refs, o_ref):
    zeros = jnp.zeros((HEAD_DIM, GROUP_HEADS * TQ), BF16)
    q0 = jnp.concatenate([qt_ref[h * HEAD_DIM:(h + 1) * HEAD_DIM, :] for h in range(GROUP_HEADS)], axis=1)
    q1 = jnp.concatenate([qt_ref[h * HEAD_DIM:(h + 1) * HEAD_DIM, :] for h in range(GROUP_HEADS, N_HEADS)], axis=1)
    w_all = jnp.concatenate([jnp.concatenate([q0, zeros], axis=0), jnp.concatenate([zeros, q1], axis=0)], axis=1)

    chunks = [(k_ref, vt_ref, c * KEY_CHUNK, min(KEY_CHUNK, k_ref.shape[0] - c * KEY_CHUNK))
              for k_ref, vt_ref in zip(k_refs, vt_refs)
              for c in range(pl.cdiv(k_ref.shape[0], KEY_CHUNK))]
    m = None
    acc = [None] * N_HEADS
    for k_ref, vt_ref, start, size in chunks:
        s = _dot(k_ref[start:start + size, :], w_all)
        m_chunk = jnp.max(s, axis=0, keepdims=True)
        m_new = m_chunk if m is None else jnp.maximum(m, m_chunk)
        p = jnp.exp2(s - m_new).astype(BF16)
        alpha = None if m is None else jnp.exp2(m - m_new)
        for h in range(N_HEADS):
            base = (h // GROUP_HEADS) * V_ROWS
            pv = _dot(vt_ref[base:base + V_ROWS, start:start + size], p[:, h * TQ:(h + 1) * TQ])
            acc[h] = pv if m is None else acc[h] * alpha[:, h * TQ:(h + 1) * TQ] + pv
        m = m_new
    outs = [a[:HEAD_DIM, :] / a[HEAD_DIM:HEAD_DIM + 1, :] for a in acc]
    o_ref[...] = jnp.concatenate(outs, axis=0).T.astype(BF16)


def _attn_kernel(qt_ref, kl_ref, kc_ref, vtl_ref, vtc_ref, o_ref, *, with_ctx_queries):
    if with_ctx_queries:
        i = pl.program_id(1)

        @pl.when(i < SEQ // TQ)
        def _():
            _attend(qt_ref, [kl_ref, kc_ref], [vtl_ref, vtc_ref], o_ref)

        @pl.when(i == SEQ // TQ)
        def _():
            _attend(qt_ref, [kc_ref], [vtc_ref], o_ref)
    else:
        _attend(qt_ref, [kl_ref, kc_ref], [vtl_ref, vtc_ref], o_ref)


def _attention(qt, k_lat, vt_lat, k_ctx, vt_ctx, *, ctx_block, with_ctx_queries):
    per_batch = SEQ // TQ
    n_q = per_batch + (1 if with_ctx_queries else 0)
    rows = ALL_ROWS if with_ctx_queries else LAT_ROWS

    def q_block(b, i):
        return jnp.where(i < per_batch, b * per_batch + i, LAT_ROWS // TQ + b)

    return pl.pallas_call(
        functools.partial(_attn_kernel, with_ctx_queries=with_ctx_queries),
        grid=(BATCH, n_q),
        in_specs=[
            pl.BlockSpec((ATTN_WIDTH, TQ), lambda b, i: (0, q_block(b, i))),
            pl.BlockSpec((SEQ, KV_WIDTH), lambda b, i: (b, 0)),
            pl.BlockSpec((CTX_LEN, KV_WIDTH), lambda b, i: (ctx_block + b, 0)),
            pl.BlockSpec((VT_ROWS, SEQ), lambda b, i: (0, b)),
            pl.BlockSpec((VT_ROWS, CTX_LEN), lambda b, i: (0, ctx_block + b)),
        ],
        out_specs=pl.BlockSpec((TQ, ATTN_WIDTH), lambda b, i: (q_block(b, i), 0)),
        out_shape=jax.ShapeDtypeStruct((rows, ATTN_WIDTH), BF16),
        compiler_params=_params(),
        name="attention",
    )(qt, k_lat, k_ctx, vt_lat, vt_ctx)


def _pool_branch(pool_ref, prev_ref, next_ref, xe_ref, q2_ref, q4_ref, q8_ref, wbd_ref, scale_ref, r):
    tm = TM_MERGE
    per_batch = SEQ // tm
    is_ctx = r >= LAT_ROWS // tm
    first = is_ctx | (r % per_batch == 0)
    last = is_ctx | (r % per_batch == per_batch - 1)
    x = pool_ref[...]
    xe_ref[0:POOL_HALO, :] = jnp.where(first, 0.0, prev_ref[...])
    xe_ref[POOL_HALO:POOL_HALO + tm, :] = x
    xe_ref[POOL_HALO + tm:2 * POOL_HALO + tm, :] = jnp.where(last, 0.0, next_ref[...])
    xe_ref[2 * POOL_HALO + tm:, :] = jnp.zeros((2 * POOL_HALO, POOL_WIDTH), F32)
    q2_ref[...] = xe_ref[0:tm + 24, :] + xe_ref[1:tm + 25, :]
    q4_ref[...] = q2_ref[0:tm + 16, :] + q2_ref[2:tm + 18, :]
    q8_ref[...] = q4_ref[0:tm + 8, LANES:] + q4_ref[4:tm + 12, LANES:]
    q16 = q8_ref[0:tm, :] + q8_ref[8:tm + 8, :]
    lane = lax.broadcasted_iota(jnp.int32, (1, LANES), 1)
    low = lane < POOL_GROUP_DIM
    sums = jnp.concatenate([
        jnp.where(low, q2_ref[7:7 + tm, 0:LANES], q4_ref[6:6 + tm, 0:LANES]),
        jnp.where(low, q8_ref[4:4 + tm, :], q16),
    ], axis=1)
    lane_w = lax.broadcasted_iota(jnp.int32, (1, POOL_WIDTH), 1)
    half = jnp.where(lane_w < 64, 1, jnp.where(lane_w < 128, 2, jnp.where(lane_w < 192, 4, 8)))
    seg_len = jnp.where(is_ctx, CTX_LEN, SEQ)
    pos = jnp.where(is_ctx, 0, r % per_batch) * tm + lax.broadcasted_iota(jnp.int32, (tm, 1), 0)
    cnt = jnp.minimum(pos + half, seg_len) - jnp.maximum(pos - half, 0)
    pooled = sums / cnt.astype(F32) - x
    return _dot(pooled.astype(BF16), wbd_ref[...]) * scale_ref[...]


def _gelu_tanh(x):
    return x * (0.5 * (1.0 + jnp.tanh(0.7978845608028654 * (x + 0.044715 * (x * x * x)))))


def _sgu_branch(sgu_ref, norm_ref, wcat_ref, bias_ref):
    gl = _gelu_tanh(sgu_ref[...])
    u = gl[:, :SGU_WIDTH]
    v = _rms(gl[:, SGU_WIDTH:], norm_ref[...])
    lane = lax.broadcasted_iota(jnp.int32, (1, SGU_WIDTH), 1)
    outs = []
    for c in range(TM_MERGE // CHUNK):
        vc = v[c * CHUNK:(c + 1) * CHUNK, :]
        stack = jnp.concatenate(
            [jnp.where((lane >= g * SGU_GROUP_DIM) & (lane < (g + 1) * SGU_GROUP_DIM), vc, 0.0)
             for g in range(SGU_GROUPS)], axis=0).astype(BF16)
        outs.append(_dot(wcat_ref[...], stack) + bias_ref[...])
    return u * jnp.concatenate(outs, axis=0)


def _merge_kernel(h_ref, mod_ref, g_ref, attn_ref, pool_ref, prev_ref, next_ref, sgu_ref,
                  wgate_ref, wbd_ref, pscale_ref, snorm_ref, wcat_ref, sbias_ref,
                  wbp_ref, wba_ref, wbs_ref, wout_ref, o_ref,
                  xe_ref, q2_ref, q4_ref, q8_ref):
    x = h_ref[...]
    hn = _modulate(x, g_ref[...], mod_ref, 3).astype(BF16)
    pool_out = _pool_branch(pool_ref, prev_ref, next_ref, xe_ref, q2_ref, q4_ref, q8_ref,
                            wbd_ref, pscale_ref, pl.program_id(0))
    sgu_out = _sgu_branch(sgu_ref, snorm_ref, wcat_ref, sbias_ref)
    merged = jax.nn.sigmoid(_dot(hn, wgate_ref[:, :D_MODEL])) * _dot(pool_out.astype(BF16), wbp_ref[...])
    merged += jax.nn.sigmoid(_dot(hn, wgate_ref[:, D_MODEL:2 * D_MODEL])) * _dot(attn_ref[...], wba_ref[...])
    merged += jax.nn.sigmoid(_dot(hn, wgate_ref[:, 2 * D_MODEL:])) * _dot(sgu_out.astype(BF16), wbs_ref[...])
    o_ref[...] = x + mod_ref[5:6, :] * _dot(merged.astype(BF16), wout_ref[...])


def _merge(h, mod, g, attn, pool_in, sgu_in, wgate, wbd, pscale, snorm, wcat, sbias, wbp, wba, wbs, wout,
           *, rows):
    tm = TM_MERGE
    blocks_per_tile = tm // POOL_HALO
    last_block = rows // POOL_HALO - 1
    prev_spec = pl.BlockSpec((POOL_HALO, POOL_WIDTH), lambda r: (jnp.maximum(r * blocks_per_tile - 1, 0), 0))
    next_spec = pl.BlockSpec((POOL_HALO, POOL_WIDTH),
                             lambda r: (jnp.minimum((r + 1) * blocks_per_tile, last_block), 0))
    return pl.pallas_call(
        _merge_kernel,
        grid=(rows // tm,),
        in_specs=[
            _row_spec(tm, D_MODEL),
            _mod_spec(tm),
            _const_spec((1, D_MODEL)),
            _row_spec(tm, ATTN_WIDTH),
            _row_spec(tm, POOL_WIDTH),
            prev_spec,
            next_spec,
            _row_spec(tm, 2 * SGU_WIDTH),
            _const_spec((D_MODEL, 3 * D_MODEL)),
            _const_spec((POOL_WIDTH, POOL_WIDTH)),
            _const_spec((1, POOL_WIDTH)),
            _const_spec((1, SGU_WIDTH)),
            _const_spec((CHUNK, SGU_GROUPS * CHUNK)),
            _const_spec((CHUNK, SGU_WIDTH)),
            _const_spec((POOL_WIDTH, D_MODEL)),
            _const_spec((ATTN_WIDTH, D_MODEL)),
            _const_spec((SGU_WIDTH, D_MODEL)),
            _const_spec((D_MODEL, D_MODEL)),
        ],
        out_specs=_row_spec(tm, D_MODEL),
        out_shape=jax.ShapeDtypeStruct((rows, D_MODEL), F32),
        scratch_shapes=[
            pltpu.VMEM((tm + 32, POOL_WIDTH), F32),
            pltpu.VMEM((tm + 24, POOL_WIDTH), F32),
            pltpu.VMEM((tm + 16, POOL_WIDTH), F32),
            pltpu.VMEM((tm + 8, LANES), F32),
        ],
        compiler_params=_params(),
        name="merge",
    )(h, mod, g, attn, pool_in, pool_in, pool_in, sgu_in, wgate, wbd, pscale, snorm, wcat, sbias,
      wbp, wba, wbs, wout)


def _rope_tables():
    pos = jnp.arange(SEQ, dtype=jnp.int32)
    row = (pos // GRID_W).astype(F32)[:, None]
    col = (pos % GRID_W).astype(F32)[:, None]
    half = HEAD_DIM // 2
    inv_freq = ROPE_THETA ** (-jnp.arange(0, half, 2, dtype=F32) / half)[None, :]
    ang = jnp.concatenate([row * inv_freq, row * inv_freq, col * inv_freq, col * inv_freq], axis=1)
    sign = jnp.tile(jnp.concatenate([-jnp.ones((half // 2,), F32), jnp.ones((half // 2,), F32)]), 2)[None, :]
    cos = jnp.concatenate([jnp.cos(ang), jnp.ones((TM_PROJ, HEAD_DIM), F32)], axis=0)
    sin = jnp.concatenate([jnp.sin(ang) * sign, jnp.zeros((TM_PROJ, HEAD_DIM), F32)], axis=0)
    return jnp.tile(cos, (1, LANES // HEAD_DIM)), jnp.tile(sin, (1, LANES // HEAD_DIM))


def _block_diag(blocks):
    n, r, c = blocks.shape
    eye = jnp.eye(n, dtype=blocks.dtype)
    return (eye[:, None, :, None] * blocks[:, :, None, :]).reshape(n * r, n * c)


def kernel(x, c, ctx, c_ctx, w_ada, b_ada, norm_ffn1, ffn1_w13, ffn1_w2, norm_mix, w_in, q_norm, k_norm,
           pool_w, pool_scale, sgu_norm, sgu_w, sgu_b, w_br_pool, w_br_attn, w_br_sgu, w_out, norm_ffn2,
           ffn2_w13, ffn2_w2, final_norm):
    cond = jnp.concatenate([c, c_ctx[None, :], jnp.zeros((COND_ROWS - BATCH - 1, D_MODEL), F32)], axis=0)
    mods = _adaln(cond, w_ada, b_ada).reshape(DEPTH, COND_ROWS, N_MOD, D_MODEL)
    cos, sin = _rope_tables()
    gmat = _block_diag(jnp.full((LANES // HEAD_DIM, HEAD_DIM, HEAD_DIM), 1.0 / HEAD_DIM, F32)).astype(BF16)
    fg = final_norm.reshape(1, D_MODEL)
    lat_tiles = LAT_ROWS // TM_PROJ
    ctx_tiles = CTX_ROWS // TM_PROJ

    hs = (x.reshape(LAT_ROWS, D_MODEL), ctx.reshape(CTX_ROWS, D_MODEL))
    for l in range(DEPTH):
        last = l == DEPTH - 1
        mod = mods[l]
        g_mix = norm_mix[l].reshape(1, D_MODEL)
        qg = jnp.tile(q_norm[l], LANES // HEAD_DIM).reshape(1, LANES)
        kg = jnp.tile(k_norm[l], LANES // HEAD_DIM).reshape(1, LANES)
        h = _ffn(hs, mod, norm_ffn1[l].reshape(1, D_MODEL), ffn1_w13[l].astype(BF16), ffn1_w2[l].astype(BF16),
                 fg, mod_base=0, rows=ALL_ROWS, final=False)
        w_proj = w_in[l, :, :OFF_GATE].astype(BF16)
        if last:
            qt, k, vt, pool_in, sgu_in = _proj(h, mod, g_mix, w_proj, qg, kg, cos, sin, gmat,
                                               first_tile=0, n_tiles=lat_tiles, kv_only=False)
            k_ctx, vt_ctx = _proj(h, mod, g_mix, w_in[l, :, OFF_K:OFF_POOL].astype(BF16), None, kg, cos, sin, gmat,
                                  first_tile=lat_tiles, n_tiles=ctx_tiles, kv_only=True)
            attn = _attention(qt, k, vt, k_ctx, vt_ctx, ctx_block=0, with_ctx_queries=False)
            rows = LAT_ROWS
        else:
            qt, k, vt, pool_in, sgu_in = _proj(h, mod, g_mix, w_proj, qg, kg, cos, sin, gmat,
                                               first_tile=0, n_tiles=lat_tiles + ctx_tiles, kv_only=False)
            attn = _attention(qt, k, vt, k, vt, ctx_block=LAT_ROWS // CTX_LEN, with_ctx_queries=True)
            rows = ALL_ROWS
        h = _merge(
            h, mod, g_mix, attn, pool_in, sgu_in, w_in[l, :, OFF_GATE:].astype(BF16),
            _block_diag(pool_w[l]).astype(BF16), pool_scale[l].reshape(1, POOL_WIDTH),
            sgu_norm[l].reshape(1, SGU_WIDTH),
            jnp.transpose(sgu_w[l], (1, 0, 2)).reshape(CHUNK, SGU_GROUPS * CHUNK).astype(BF16),
            jnp.repeat(sgu_b[l].T, SGU_GROUP_DIM, axis=1),
            w_br_pool[l].astype(BF16), w_br_attn[l].astype(BF16), w_br_sgu[l].astype(BF16),
            w_out[l].astype(BF16), rows=rows)
        hs = (_ffn((h,), mod, norm_ffn2[l].reshape(1, D_MODEL), ffn2_w13[l].astype(BF16),
                   ffn2_w2[l].astype(BF16), fg, mod_base=6, rows=rows, final=last),)
    return hs[0].reshape(BATCH, SEQ, D_MODEL)
```

```python
import functools

import jax
import jax.numpy as jnp
from jax import lax
from jax.experimental import pallas as pl
from jax.experimental.pallas import tpu as pltpu

D_MODEL = 1024
BATCH = 16
SEQ = 2048
DEPTH = 2
GRID_W = 64
CTX_LEN = 256
N_HEADS = 8
N_KV_HEADS = 2
HEAD_DIM = 64
ATTN_WIDTH = N_HEADS * HEAD_DIM
KV_WIDTH = N_KV_HEADS * HEAD_DIM
ROPE_THETA = 10000.0
POOL_WINDOWS = (2, 4, 8, 16)
POOL_GROUPS = 4
POOL_GROUP_DIM = 64
POOL_WIDTH = POOL_GROUPS * POOL_GROUP_DIM
SGU_GROUPS = 4
SGU_GROUP_DIM = 64
SGU_WIDTH = SGU_GROUPS * SGU_GROUP_DIM
CHUNK = 128
D_FF = 2816
EPS = 1e-6
N_MOD = 9
OFF_K = ATTN_WIDTH
OFF_POOL = ATTN_WIDTH + 2 * KV_WIDTH
OFF_GATE = OFF_POOL + POOL_WIDTH + 2 * SGU_WIDTH

LAT_ROWS = BATCH * SEQ
CTX_ROWS = BATCH * CTX_LEN
ALL_ROWS = LAT_ROWS + CTX_ROWS
GROUP_HEADS = N_HEADS // N_KV_HEADS
LANES = 128
COND_ROWS = 24
POOL_HALO = max(POOL_WINDOWS) // 2
BF16_ROWS = 16
V_ROWS = HEAD_DIM + BF16_ROWS
VT_ROWS = N_KV_HEADS * V_ROWS
Q_SCALE = HEAD_DIM ** -0.5 * 1.4426950408889634
SUB = 256
TM_FFN = 1024
TM_PROJ = 1024
TM_MERGE = 1024
TQ = CTX_LEN
KEY_CHUNK = 256
VMEM_LIMIT = 56 * 1024 * 1024

F32 = jnp.float32
BF16 = jnp.bfloat16


def _dot(a, b):
    return jnp.dot(a, b, preferred_element_type=F32)


def _rms(x, g):
    return x * lax.rsqrt(jnp.mean(x * x, axis=-1, keepdims=True) + EPS) * g


def _modulate(x, g, mod_ref, base):
    shift = mod_ref[base:base + 1, :]
    scale = mod_ref[base + 1:base + 2, :]
    return _rms(x, g) * (1.0 + scale) + shift


def _const_spec(shape):
    zeros = (0,) * len(shape)
    return pl.BlockSpec(shape, lambda *_: zeros, pipeline_mode=pl.Buffered(1))


def _row_spec(tm, width):
    return pl.BlockSpec((tm, width), lambda r: (r, 0))


def _col_spec(height, tm):
    return pl.BlockSpec((height, tm), lambda r: (0, r))


def _mod_spec(tm):
    lat_tiles = LAT_ROWS // tm
    per_batch = SEQ // tm
    return pl.BlockSpec((None, N_MOD, D_MODEL),
                        lambda r: (jnp.where(r >= lat_tiles, BATCH, r // per_batch), 0, 0))


def _params():
    return pltpu.CompilerParams(vmem_limit_bytes=VMEM_LIMIT)


def _software_pipeline(stages, n_sub):
    carry = [None] * n_sub
    for step in range(n_sub + len(stages) - 1):
        for s in reversed(range(len(stages))):
            j = step - s
            if 0 <= j < n_sub:
                carry[j] = stages[s](j, carry[j])


def _adaln_kernel(cond_ref, w_ref, b_ref, o_ref):
    cond = cond_ref[...]
    s = (cond * jax.nn.sigmoid(cond)).astype(BF16)
    o_ref[...] = _dot(s, w_ref[...].astype(BF16)) + b_ref[...]


def _adaln(cond, w_ada, b_ada):
    return pl.pallas_call(
        _adaln_kernel,
        grid=(DEPTH, N_MOD),
        in_specs=[
            pl.BlockSpec((COND_ROWS, D_MODEL), lambda l, j: (0, 0)),
            pl.BlockSpec((None, D_MODEL, D_MODEL), lambda l, j: (l, 0, j)),
            pl.BlockSpec((None, 1, D_MODEL), lambda l, j: (l, 0, j)),
        ],
        out_specs=pl.BlockSpec((None, COND_ROWS, D_MODEL), lambda l, j: (l, 0, j)),
        out_shape=jax.ShapeDtypeStruct((DEPTH, COND_ROWS, N_MOD * D_MODEL), F32),
        compiler_params=_params(),
        name="adaln",
    )(cond, w_ada, b_ada.reshape(DEPTH, 1, N_MOD * D_MODEL))


def _ffn_kernel(*refs, mod_base, final, split_input):
    if split_input:
        lat_ref, ctx_ref, mod_ref, g_ref, w13_ref, w2_ref, fg_ref, o_ref = refs
        is_lat = pl.program_id(0) < LAT_ROWS // TM_FFN
    else:
        h_ref, mod_ref, g_ref, w13_ref, w2_ref, fg_ref, o_ref = refs

    def load(j):
        rows = pl.ds(j * SUB, SUB)
        if split_input:
            return jnp.where(is_lat, lat_ref[rows, :], ctx_ref[rows, :])
        return h_ref[rows, :]

    def norm(j, _):
        return _modulate(load(j), g_ref[...], mod_ref, mod_base).astype(BF16)

    def up(j, xn):
        return _dot(xn, w13_ref[:, :D_FF]), _dot(xn, w13_ref[:, D_FF:])

    def act(j, ab):
        a, b = ab
        return (a * jax.nn.sigmoid(a) * b).astype(BF16)

    def down(j, u):
        return _dot(u, w2_ref[...])

    def store(j, y):
        out = load(j) + (0.5 * mod_ref[mod_base + 2:mod_base + 3, :]) * y
        if final:
            out = _rms(out, fg_ref[...])
        o_ref[pl.ds(j * SUB, SUB), :] = out

    _software_pipeline([norm, up, act, down, store], TM_FFN // SUB)


def _ffn(hs, mod, g, w13, w2, fg, *, mod_base, rows, final):
    split_input = len(hs) == 2
    lat_tiles = LAT_ROWS // TM_FFN
    if split_input:
        h_specs = [pl.BlockSpec((TM_FFN, D_MODEL), lambda r: (jnp.minimum(r, lat_tiles - 1), 0)),
                   pl.BlockSpec((TM_FFN, D_MODEL), lambda r: (jnp.maximum(r - lat_tiles, 0), 0))]
    else:
        h_specs = [_row_spec(TM_FFN, D_MODEL)]
    return pl.pallas_call(
        functools.partial(_ffn_kernel, mod_base=mod_base, final=final, split_input=split_input),
        grid=(rows // TM_FFN,),
        in_specs=h_specs + [
            _mod_spec(TM_FFN),
            _const_spec((1, D_MODEL)),
            _const_spec((D_MODEL, 2 * D_FF)),
            _const_spec((D_FF, D_MODEL)),
            _const_spec((1, D_MODEL)),
        ],
        out_specs=_row_spec(TM_FFN, D_MODEL),
        out_shape=jax.ShapeDtypeStruct((rows, D_MODEL), F32),
        compiler_params=_params(),
        name="ffn",
    )(*hs, mod, g, w13, w2, fg)


def _head_norm_rope(x, gmat, g, cos, sin, first_half):
    ms = _dot((x * x).astype(BF16), gmat)
    y = x * lax.rsqrt(ms + EPS) * g
    up = pltpu.roll(y, LANES - HEAD_DIM // 4, 1)
    dn = pltpu.roll(y, HEAD_DIM // 4, 1)
    return y * cos + jnp.where(first_half, up, dn) * sin


def _proj_kernel(*refs, kv_only):
    if kv_only:
        h_ref, mod_ref, g_ref, w_ref, kg_ref, cos_ref, sin_ref, gmat_ref, k_ref, vt_ref = refs
    else:
        (h_ref, mod_ref, g_ref, w_ref, qg_ref, kg_ref, cos_ref, sin_ref, gmat_ref,
         qt_ref, k_ref, vt_ref, pool_ref, sgu_ref) = refs
    lane = lax.broadcasted_iota(jnp.int32, (1, LANES), 1)
    first_half = (lane % (HEAD_DIM // 2)) < (HEAD_DIM // 4)
    base = 0 if kv_only else OFF_K

    def norm(j, _):
        return _modulate(h_ref[pl.ds(j * SUB, SUB), :], g_ref[...], mod_ref, 3).astype(BF16)

    def project(j, hn):
        return _dot(hn, w_ref[...])

    def finish(j, p):
        rows = pl.ds(j * SUB, SUB)
        gmat, cos, sin = gmat_ref[...], cos_ref[rows, :], sin_ref[rows, :]
        if not kv_only:
            for c in range(ATTN_WIDTH // LANES):
                qc = _head_norm_rope(p[:, c * LANES:(c + 1) * LANES], gmat, qg_ref[...], cos, sin, first_half)
                qt_ref[c * LANES:(c + 1) * LANES, rows] = (qc * Q_SCALE).T.astype(BF16)
            pool_ref[rows, :] = p[:, OFF_POOL:OFF_POOL + POOL_WIDTH]
            sgu_ref[rows, :] = p[:, OFF_POOL + POOL_WIDTH:OFF_GATE]
        k = _head_norm_rope(p[:, base:base + KV_WIDTH], gmat, kg_ref[...], cos, sin, first_half)
        k_ref[rows, :] = k.astype(BF16)
        vt = p[:, base + KV_WIDTH:base + 2 * KV_WIDTH].T.astype(BF16)
        ones = jnp.ones((V_ROWS - HEAD_DIM, SUB), BF16)
        for g in range(N_KV_HEADS):
            vt_ref[g * V_ROWS:g * V_ROWS + HEAD_DIM, rows] = vt[g * HEAD_DIM:(g + 1) * HEAD_DIM, :]
            vt_ref[g * V_ROWS + HEAD_DIM:(g + 1) * V_ROWS, rows] = ones

    _software_pipeline([norm, project, finish], TM_PROJ // SUB)


def _proj(h, mod, g, w, qg, kg, cos, sin, gmat, *, first_tile, n_tiles, kv_only):
    tm = TM_PROJ
    lat_tiles = LAT_ROWS // tm
    per_batch = SEQ // tm
    rows = n_tiles * tm
    h_spec = pl.BlockSpec((tm, D_MODEL), lambda r: (r + first_tile, 0))
    mod_spec = pl.BlockSpec(
        (None, N_MOD, D_MODEL),
        lambda r: (jnp.where(r + first_tile >= lat_tiles, BATCH, (r + first_tile) // per_batch), 0, 0))
    tab_spec = pl.BlockSpec(
        (tm, LANES), lambda r: (jnp.where(r + first_tile >= lat_tiles, per_batch, (r + first_tile) % per_batch), 0))
    gains = [_const_spec((1, LANES))] * (1 if kv_only else 2)
    in_specs = [h_spec, mod_spec, _const_spec((1, D_MODEL)), _const_spec(w.shape)] + gains + [
        tab_spec, tab_spec, _const_spec((LANES, LANES))]
    out_specs = [_row_spec(tm, KV_WIDTH), _col_spec(VT_ROWS, tm)]
    out_shape = [jax.ShapeDtypeStruct((rows, KV_WIDTH), BF16), jax.ShapeDtypeStruct((VT_ROWS, rows), BF16)]
    if not kv_only:
        out_specs = [_col_spec(ATTN_WIDTH, tm)] + out_specs + [_row_spec(tm, POOL_WIDTH), _row_spec(tm, 2 * SGU_WIDTH)]
        out_shape = ([jax.ShapeDtypeStruct((ATTN_WIDTH, rows), BF16)] + out_shape
                     + [jax.ShapeDtypeStruct((rows, POOL_WIDTH), F32),
                        jax.ShapeDtypeStruct((rows, 2 * SGU_WIDTH), F32)])
    args = (h, mod, g, w) + ((kg,) if kv_only else (qg, kg)) + (cos, sin, gmat)
    return pl.pallas_call(
        functools.partial(_proj_kernel, kv_only=kv_only),
        grid=(n_tiles,),
        in_specs=in_specs,
        out_specs=out_specs,
        out_shape=out_shape,
        compiler_params=_params(),
        name="proj_kv" if kv_only else "proj",
    )(*args)


def _attend(qt_ref, k_refs, vt_refs, o_ref):
    zeros = jnp.zeros((HEAD_DIM, GROUP_HEADS * TQ), BF16)
    q0 = jnp.concatenate([qt_ref[h * HEAD_DIM:(h + 1) * HEAD_DIM, :] for h in range(GROUP_HEADS)], axis=1)
    q1 = jnp.concatenate([qt_ref[h * HEAD_DIM:(h + 1) * HEAD_DIM, :] for h in range(GROUP_HEADS, N_HEADS)], axis=1)
    w_all = jnp.concatenate([jnp.concatenate([q0, zeros], axis=0), jnp.concatenate([zeros, q1], axis=0)], axis=1)

    chunks = [(k_ref, vt_ref, c * KEY_CHUNK, min(KEY_CHUNK, k_ref.shape[0] - c * KEY_CHUNK))
              for k_ref, vt_ref in zip(k_refs, vt_refs)
              for c in range(pl.cdiv(k_ref.shape[0], KEY_CHUNK))]
    m = None
    acc = [None] * N_HEADS
    for k_ref, vt_ref, start, size in chunks:
        s = _dot(k_ref[start:start + size, :], w_all)
        m_chunk = jnp.max(s, axis=0, keepdims=True)
        m_new = m_chunk if m is None else jnp.maximum(m, m_chunk)
        p = jnp.exp2(s - m_new).astype(BF16)
        alpha = None if m is None else jnp.exp2(m - m_new)
        for h in range(N_HEADS):
            base = (h // GROUP_HEADS) * V_ROWS
            pv = _dot(vt_ref[base:base + V_ROWS, start:start + size], p[:, h * TQ:(h + 1) * TQ])
            acc[h] = pv if m is None else acc[h] * alpha[:, h * TQ:(h + 1) * TQ] + pv
        m = m_new
    outs = [a[:HEAD_DIM, :] / a[HEAD_DIM:HEAD_DIM + 1, :] for a in acc]
    o_ref[...] = jnp.concatenate(outs, axis=0).T.astype(BF16)


def _attn_kernel(qt_ref, kl_ref, kc_ref, vtl_ref, vtc_ref, o_ref, *, with_ctx_queries):
    if with_ctx_queries:
        i = pl.program_id(1)

        @pl.when(i < SEQ // TQ)
        def _():
            _attend(qt_ref, [kl_ref, kc_ref], [vtl_ref, vtc_ref], o_ref)

        @pl.when(i == SEQ // TQ)
        def _():
            _attend(qt_ref, [kc_ref], [vtc_ref], o_ref)
    else:
        _attend(qt_ref, [kl_ref, kc_ref], [vtl_ref, vtc_ref], o_ref)


def _attention(qt, k_lat, vt_lat, k_ctx, vt_ctx, *, ctx_block, with_ctx_queries):
    per_batch = SEQ // TQ
    n_q = per_batch + (1 if with_ctx_queries else 0)
    rows = ALL_ROWS if with_ctx_queries else LAT_ROWS

    def q_block(b, i):
        return jnp.where(i < per_batch, b * per_batch + i, LAT_ROWS // TQ + b)

    return pl.pallas_call(
        functools.partial(_attn_kernel, with_ctx_queries=with_ctx_queries),
        grid=(BATCH, n_q),
        in_specs=[
            pl.BlockSpec((ATTN_WIDTH, TQ), lambda b, i: (0, q_block(b, i))),
            pl.BlockSpec((SEQ, KV_WIDTH), lambda b, i: (b, 0)),
            pl.BlockSpec((CTX_LEN, KV_WIDTH), lambda b, i: (ctx_block + b, 0)),
            pl.BlockSpec((VT_ROWS, SEQ), lambda b, i: (0, b)),
            pl.BlockSpec((VT_ROWS, CTX_LEN), lambda b, i: (0, ctx_block + b)),
        ],
        out_specs=pl.BlockSpec((TQ, ATTN_WIDTH), lambda b, i: (q_block(b, i), 0)),
        out_shape=jax.ShapeDtypeStruct((rows, ATTN_WIDTH), BF16),
        compiler_params=_params(),
        name="attention",
    )(qt, k_lat, k_ctx, vt_lat, vt_ctx)


def _pool_branch(j, pool_ref, prev_ref, next_ref, xe_ref, q2_ref, q4_ref, q8_ref, wbd_ref, scale_ref):
    tm = SUB
    n_sub = TM_MERGE // SUB
    sub = pl.program_id(0) * n_sub + j
    is_ctx = sub >= LAT_ROWS // SUB
    per_seg = jnp.where(is_ctx, CTX_LEN // SUB, SEQ // SUB)
    seg_idx = jnp.where(is_ctx, sub - LAT_ROWS // SUB, sub) % per_seg
    first = seg_idx == 0
    last = seg_idx == per_seg - 1
    x = pool_ref[pl.ds(j * SUB, SUB), :]
    prev_rows = prev_ref[...] if j == 0 else pool_ref[pl.ds(j * SUB - POOL_HALO, POOL_HALO), :]
    next_rows = next_ref[...] if j == n_sub - 1 else pool_ref[pl.ds((j + 1) * SUB, POOL_HALO), :]
    xe_ref[0:POOL_HALO, :] = jnp.where(first, 0.0, prev_rows)
    xe_ref[POOL_HALO:POOL_HALO + tm, :] = x
    xe_ref[POOL_HALO + tm:2 * POOL_HALO + tm, :] = jnp.where(last, 0.0, next_rows)
    xe_ref[2 * POOL_HALO + tm:, :] = jnp.zeros((2 * POOL_HALO, POOL_WIDTH), F32)
    q2_ref[...] = xe_ref[0:tm + 24, :] + xe_ref[1:tm + 25, :]
    q4_ref[...] = q2_ref[0:tm + 16, :] + q2_ref[2:tm + 18, :]
    q8_ref[...] = q4_ref[0:tm + 8, LANES:] + q4_ref[4:tm + 12, LANES:]
    q16 = q8_ref[0:tm, :] + q8_ref[8:tm + 8, :]
    lane = lax.broadcasted_iota(jnp.int32, (1, LANES), 1)
    low = lane < POOL_GROUP_DIM
    sums = jnp.concatenate([
        jnp.where(low, q2_ref[7:7 + tm, 0:LANES], q4_ref[6:6 + tm, 0:LANES]),
        jnp.where(low, q8_ref[4:4 + tm, :], q16),
    ], axis=1)
    lane_w = lax.broadcasted_iota(jnp.int32, (1, POOL_WIDTH), 1)
    half = jnp.where(lane_w < 64, 1, jnp.where(lane_w < 128, 2, jnp.where(lane_w < 192, 4, 8)))
    seg_len = per_seg * SUB
    pos = seg_idx * SUB + lax.broadcasted_iota(jnp.int32, (tm, 1), 0)
    cnt = jnp.minimum(pos + half, seg_len) - jnp.maximum(pos - half, 0)
    pooled = sums / cnt.astype(F32) - x
    return _dot(pooled.astype(BF16), wbd_ref[...]) * scale_ref[...]


def _gelu_tanh(x):
    return x * (0.5 * (1.0 + jnp.tanh(0.7978845608028654 * (x + 0.044715 * (x * x * x)))))


def _sgu_branch(j, sgu_ref, norm_ref, wcat_ref, bias_ref):
    gl = _gelu_tanh(sgu_ref[pl.ds(j * SUB, SUB), :])
    u = gl[:, :SGU_WIDTH]
    v = _rms(gl[:, SGU_WIDTH:], norm_ref[...])
    lane = lax.broadcasted_iota(jnp.int32, (1, SGU_WIDTH), 1)
    outs = []
    for c in range(SUB // CHUNK):
        vc = v[c * CHUNK:(c + 1) * CHUNK, :]
        stack = jnp.concatenate(
            [jnp.where((lane >= g * SGU_GROUP_DIM) & (lane < (g + 1) * SGU_GROUP_DIM), vc, 0.0)
             for g in range(SGU_GROUPS)], axis=0).astype(BF16)
        outs.append(_dot(wcat_ref[...], stack) + bias_ref[...])
    return u * jnp.concatenate(outs, axis=0)


def _merge_kernel(h_ref, mod_ref, g_ref, attn_ref, pool_ref, prev_ref, next_ref, sgu_ref,
                  wgate_ref, wbd_ref, pscale_ref, snorm_ref, wcat_ref, sbias_ref,
                  wbp_ref, wba_ref, wbs_ref, wout_ref, o_ref,
                  xe_ref, q2_ref, q4_ref, q8_ref):
    def norm(j, _):
        return _modulate(h_ref[pl.ds(j * SUB, SUB), :], g_ref[...], mod_ref, 3).astype(BF16)

    def gates_and_mixers(j, hn):
        gates = [_dot(hn, wgate_ref[:, i * D_MODEL:(i + 1) * D_MODEL]) for i in range(3)]
        pool_out = _pool_branch(j, pool_ref, prev_ref, next_ref, xe_ref, q2_ref, q4_ref, q8_ref,
                                wbd_ref, pscale_ref)
        sgu_out = _sgu_branch(j, sgu_ref, snorm_ref, wcat_ref, sbias_ref)
        return gates, pool_out.astype(BF16), sgu_out.astype(BF16)

    def branches(j, carry):
        gates, pool_out, sgu_out = carry
        merged = jax.nn.sigmoid(gates[0]) * _dot(pool_out, wbp_ref[...])
        merged += jax.nn.sigmoid(gates[1]) * _dot(attn_ref[pl.ds(j * SUB, SUB), :], wba_ref[...])
        merged += jax.nn.sigmoid(gates[2]) * _dot(sgu_out, wbs_ref[...])
        return merged.astype(BF16)

    def store(j, merged):
        rows = pl.ds(j * SUB, SUB)
        o_ref[rows, :] = h_ref[rows, :] + mod_ref[5:6, :] * _dot(merged, wout_ref[...])

    _software_pipeline([norm, gates_and_mixers, branches, store], TM_MERGE // SUB)


def _merge(h, mod, g, attn, pool_in, sgu_in, wgate, wbd, pscale, snorm, wcat, sbias, wbp, wba, wbs, wout,
           *, rows):
    tm = TM_MERGE
    blocks_per_tile = tm // POOL_HALO
    last_block = rows // POOL_HALO - 1
    prev_spec = pl.BlockSpec((POOL_HALO, POOL_WIDTH), lambda r: (jnp.maximum(r * blocks_per_tile - 1, 0), 0))
    next_spec = pl.BlockSpec((POOL_HALO, POOL_WIDTH),
                             lambda r: (jnp.minimum((r + 1) * blocks_per_tile, last_block), 0))
    return pl.pallas_call(
        _merge_kernel,
        grid=(rows // tm,),
        in_specs=[
            _row_spec(tm, D_MODEL),
            _mod_spec(tm),
            _const_spec((1, D_MODEL)),
            _row_spec(tm, ATTN_WIDTH),
            _row_spec(tm, POOL_WIDTH),
            prev_spec,
            next_spec,
            _row_spec(tm, 2 * SGU_WIDTH),
            _const_spec((D_MODEL, 3 * D_MODEL)),
            _const_spec((POOL_WIDTH, POOL_WIDTH)),
            _const_spec((1, POOL_WIDTH)),
            _const_spec((1, SGU_WIDTH)),
            _const_spec((CHUNK, SGU_GROUPS * CHUNK)),
            _const_spec((CHUNK, SGU_WIDTH)),
            _const_spec((POOL_WIDTH, D_MODEL)),
            _const_spec((ATTN_WIDTH, D_MODEL)),
            _const_spec((SGU_WIDTH, D_MODEL)),
            _const_spec((D_MODEL, D_MODEL)),
        ],
        out_specs=_row_spec(tm, D_MODEL),
        out_shape=jax.ShapeDtypeStruct((rows, D_MODEL), F32),
        scratch_shapes=[
            pltpu.VMEM((SUB + 32, POOL_WIDTH), F32),
            pltpu.VMEM((SUB + 24, POOL_WIDTH), F32),
            pltpu.VMEM((SUB + 16, POOL_WIDTH), F32),
            pltpu.VMEM((SUB + 8, LANES), F32),
        ],
        compiler_params=_params(),
        name="merge",
    )(h, mod, g, attn, pool_in, pool_in, pool_in, sgu_in, wgate, wbd, pscale, snorm, wcat, sbias,
      wbp, wba, wbs, wout)


def _rope_tables():
    pos = jnp.arange(SEQ, dtype=jnp.int32)
    row = (pos // GRID_W).astype(F32)[:, None]
    col = (pos % GRID_W).astype(F32)[:, None]
    half = HEAD_DIM // 2
    inv_freq = ROPE_THETA ** (-jnp.arange(0, half, 2, dtype=F32) / half)[None, :]
    ang = jnp.concatenate([row * inv_freq, row * inv_freq, col * inv_freq, col * inv_freq], axis=1)
    sign = jnp.tile(jnp.concatenate([-jnp.ones((half // 2,), F32), jnp.ones((half // 2,), F32)]), 2)[None, :]
    cos = jnp.concatenate([jnp.cos(ang), jnp.ones((TM_PROJ, HEAD_DIM), F32)], axis=0)
    sin = jnp.concatenate([jnp.sin(ang) * sign, jnp.zeros((TM_PROJ, HEAD_DIM), F32)], axis=0)
    return jnp.tile(cos, (1, LANES // HEAD_DIM)), jnp.tile(sin, (1, LANES // HEAD_DIM))


def _block_diag(blocks):
    n, r, c = blocks.shape
    eye = jnp.eye(n, dtype=blocks.dtype)
    return (eye[:, None, :, None] * blocks[:, :, None, :]).reshape(n * r, n * c)


def kernel(x, c, ctx, c_ctx, w_ada, b_ada, norm_ffn1, ffn1_w13, ffn1_w2, norm_mix, w_in, q_norm, k_norm,
           pool_w, pool_scale, sgu_norm, sgu_w, sgu_b, w_br_pool, w_br_attn, w_br_sgu, w_out, norm_ffn2,
           ffn2_w13, ffn2_w2, final_norm):
    cond = jnp.concatenate([c, c_ctx[None, :], jnp.zeros((COND_ROWS - BATCH - 1, D_MODEL), F32)], axis=0)
    mods = _adaln(cond, w_ada, b_ada).reshape(DEPTH, COND_ROWS, N_MOD, D_MODEL)
    cos, sin = _rope_tables()
    gmat = _block_diag(jnp.full((LANES // HEAD_DIM, HEAD_DIM, HEAD_DIM), 1.0 / HEAD_DIM, F32)).astype(BF16)
    fg = final_norm.reshape(1, D_MODEL)
    lat_tiles = LAT_ROWS // TM_PROJ
    ctx_tiles = CTX_ROWS // TM_PROJ

    hs = (x.reshape(LAT_ROWS, D_MODEL), ctx.reshape(CTX_ROWS, D_MODEL))
    for l in range(DEPTH):
        last = l == DEPTH - 1
        mod = mods[l]
        g_mix = norm_mix[l].reshape(1, D_MODEL)
        qg = jnp.tile(q_norm[l], LANES // HEAD_DIM).reshape(1, LANES)
        kg = jnp.tile(k_norm[l], LANES // HEAD_DIM).reshape(1, LANES)
        h = _ffn(hs, mod, norm_ffn1[l].reshape(1, D_MODEL), ffn1_w13[l].astype(BF16), ffn1_w2[l].astype(BF16),
                 fg, mod_base=0, rows=ALL_ROWS, final=False)
        w_proj = w_in[l, :, :OFF_GATE].astype(BF16)
        if last:
            qt, k, vt, pool_in, sgu_in = _proj(h, mod, g_mix, w_proj, qg, kg, cos, sin, gmat,
                                               first_tile=0, n_tiles=lat_tiles, kv_only=False)
            k_ctx, vt_ctx = _proj(h, mod, g_mix, w_in[l, :, OFF_K:OFF_POOL].astype(BF16), None, kg, cos, sin, gmat,
                                  first_tile=lat_tiles, n_tiles=ctx_tiles, kv_only=True)
            attn = _attention(qt, k, vt, k_ctx, vt_ctx, ctx_block=0, with_ctx_queries=False)
            rows = LAT_ROWS
        else:
            qt, k, vt, pool_in, sgu_in = _proj(h, mod, g_mix, w_proj, qg, kg, cos, sin, gmat,
                                               first_tile=0, n_tiles=lat_tiles + ctx_tiles, kv_only=False)
            attn = _attention(qt, k, vt, k, vt, ctx_block=LAT_ROWS // CTX_LEN, with_ctx_queries=True)
            rows = ALL_ROWS
        h = _merge(
            h, mod, g_mix, attn, pool_in, sgu_in, w_in[l, :, OFF_GATE:].astype(BF16),
            _block_diag(pool_w[l]).astype(BF16), pool_scale[l].reshape(1, POOL_WIDTH),
            sgu_norm[l].reshape(1, SGU_WIDTH),
            jnp.transpose(sgu_w[l], (1, 0, 2)).reshape(CHUNK, SGU_GROUPS * CHUNK).astype(BF16),
            jnp.repeat(sgu_b[l].T, SGU_GROUP_DIM, axis=1),
            w_br_pool[l].astype(BF16), w_br_attn[l].astype(BF16), w_br_sgu[l].astype(BF16),
            w_out[l].astype(BF16), rows=rows)
        hs = (_ffn((h,), mod, norm_ffn2[l].reshape(1, D_MODEL), ffn2_w13[l].astype(BF16),
                   ffn2_w2[l].astype(BF16), fg, mod_base=6, rows=rows, final=last),)
    return hs[0].reshape(BATCH, SEQ, D_MODEL)
```

```python
import functools

import jax
import jax.numpy as jnp
from jax import lax
from jax.experimental import pallas as pl
from jax.experimental.pallas import tpu as pltpu

D_MODEL = 1024
BATCH = 16
SEQ = 2048
DEPTH = 2
GRID_W = 64
CTX_LEN = 256
N_HEADS = 8
N_KV_HEADS = 2
HEAD_DIM = 64
ATTN_WIDTH = N_HEADS * HEAD_DIM
KV_WIDTH = N_KV_HEADS * HEAD_DIM
ROPE_THETA = 10000.0
POOL_WINDOWS = (2, 4, 8, 16)
POOL_GROUPS = 4
POOL_GROUP_DIM = 64
POOL_WIDTH = POOL_GROUPS * POOL_GROUP_DIM
SGU_GROUPS = 4
SGU_GROUP_DIM = 64
SGU_WIDTH = SGU_GROUPS * SGU_GROUP_DIM
CHUNK = 128
D_FF = 2816
EPS = 1e-6
N_MOD = 9
OFF_K = ATTN_WIDTH
OFF_POOL = ATTN_WIDTH + 2 * KV_WIDTH
OFF_GATE = OFF_POOL + POOL_WIDTH + 2 * SGU_WIDTH

LAT_ROWS = BATCH * SEQ
CTX_ROWS = BATCH * CTX_LEN
ALL_ROWS = LAT_ROWS + CTX_ROWS
GROUP_HEADS = N_HEADS // N_KV_HEADS
LANES = 128
COND_ROWS = 24
POOL_HALO = max(POOL_WINDOWS) // 2
BF16_ROWS = 16
V_ROWS = HEAD_DIM + BF16_ROWS
VT_ROWS = N_KV_HEADS * V_ROWS
Q_SCALE = HEAD_DIM ** -0.5 * 1.4426950408889634
SUB = 256
TM_FFN = 1024
TM_PROJ = 1024
TM_MERGE = 1024
TQ = CTX_LEN
KEY_CHUNK = 256
VMEM_LIMIT = 56 * 1024 * 1024

F32 = jnp.float32
BF16 = jnp.bfloat16


def _dot(a, b):
    return jnp.dot(a, b, preferred_element_type=F32)


def _rms(x, g):
    return x * lax.rsqrt(jnp.mean(x * x, axis=-1, keepdims=True) + EPS) * g


def _modulate(x, g, mod_ref, base):
    shift = mod_ref[base:base + 1, :]
    scale = mod_ref[base + 1:base + 2, :]
    return _rms(x, g) * (1.0 + scale) + shift


def _const_spec(shape):
    zeros = (0,) * len(shape)
    return pl.BlockSpec(shape, lambda *_: zeros, pipeline_mode=pl.Buffered(1))


def _layer_spec(shape, layer):
    index = (layer,) + (0,) * len(shape)
    return pl.BlockSpec((None,) + tuple(shape), lambda *_: index, pipeline_mode=pl.Buffered(1))


def _row_spec(tm, width):
    return pl.BlockSpec((tm, width), lambda r: (r, 0))


def _col_spec(height, tm):
    return pl.BlockSpec((height, tm), lambda r: (0, r))


def _mod_spec(tm):
    lat_tiles = LAT_ROWS // tm
    per_batch = SEQ // tm
    return pl.BlockSpec((None, N_MOD, D_MODEL),
                        lambda r: (jnp.where(r >= lat_tiles, BATCH, r // per_batch), 0, 0))


def _params():
    return pltpu.CompilerParams(vmem_limit_bytes=VMEM_LIMIT)


def _software_pipeline(stages, n_sub):
    carry = [None] * n_sub
    for step in range(n_sub + len(stages) - 1):
        for s in reversed(range(len(stages))):
            j = step - s
            if 0 <= j < n_sub:
                carry[j] = stages[s](j, carry[j])


def _adaln_kernel(cond_ref, w_ref, b_ref, o_ref):
    cond = cond_ref[...]
    s = (cond * jax.nn.sigmoid(cond)).astype(BF16)
    o_ref[...] = _dot(s, w_ref[...].astype(BF16)) + b_ref[...]


def _adaln(cond, w_ada, b_ada):
    return pl.pallas_call(
        _adaln_kernel,
        grid=(DEPTH, N_MOD),
        in_specs=[
            pl.BlockSpec((COND_ROWS, D_MODEL), lambda l, j: (0, 0)),
            pl.BlockSpec((None, D_MODEL, D_MODEL), lambda l, j: (l, 0, j)),
            pl.BlockSpec((None, 1, D_MODEL), lambda l, j: (l, 0, j)),
        ],
        out_specs=pl.BlockSpec((None, COND_ROWS, D_MODEL), lambda l, j: (l, 0, j)),
        out_shape=jax.ShapeDtypeStruct((DEPTH, COND_ROWS, N_MOD * D_MODEL), F32),
        compiler_params=_params(),
        name="adaln",
    )(cond, w_ada, b_ada.reshape(DEPTH, 1, N_MOD * D_MODEL))


def _ffn_kernel(*refs, mod_base, final, split_input):
    if split_input:
        lat_ref, ctx_ref, mod_ref, g_ref, w13_ref, w2_ref, fg_ref, o_ref = refs
        is_lat = pl.program_id(0) < LAT_ROWS // TM_FFN
    else:
        h_ref, mod_ref, g_ref, w13_ref, w2_ref, fg_ref, o_ref = refs

    def load(j):
        rows = pl.ds(j * SUB, SUB)
        if split_input:
            return jnp.where(is_lat, lat_ref[rows, :], ctx_ref[rows, :])
        return h_ref[rows, :]

    def norm(j, _):
        return _modulate(load(j), g_ref[...], mod_ref, mod_base).astype(BF16)

    def up(j, xn):
        return _dot(xn, w13_ref[:, :D_FF]), _dot(xn, w13_ref[:, D_FF:])

    def act(j, ab):
        a, b = ab
        return (a * jax.nn.sigmoid(a) * b).astype(BF16)

    def down(j, u):
        return _dot(u, w2_ref[...])

    def store(j, y):
        out = load(j) + (0.5 * mod_ref[mod_base + 2:mod_base + 3, :]) * y
        if final:
            out = _rms(out, fg_ref[...])
        o_ref[pl.ds(j * SUB, SUB), :] = out

    _software_pipeline([norm, up, act, down, store], TM_FFN // SUB)


def _ffn(hs, mod, g, w13, w2, fg, *, layer, mod_base, rows, final):
    split_input = len(hs) == 2
    lat_tiles = LAT_ROWS // TM_FFN
    if split_input:
        h_specs = [pl.BlockSpec((TM_FFN, D_MODEL), lambda r: (jnp.minimum(r, lat_tiles - 1), 0)),
                   pl.BlockSpec((TM_FFN, D_MODEL), lambda r: (jnp.maximum(r - lat_tiles, 0), 0))]
    else:
        h_specs = [_row_spec(TM_FFN, D_MODEL)]
    return pl.pallas_call(
        functools.partial(_ffn_kernel, mod_base=mod_base, final=final, split_input=split_input),
        grid=(rows // TM_FFN,),
        in_specs=h_specs + [
            _mod_spec(TM_FFN),
            _const_spec((1, D_MODEL)),
            _layer_spec((D_MODEL, 2 * D_FF), layer),
            _layer_spec((D_FF, D_MODEL), layer),
            _const_spec((1, D_MODEL)),
        ],
        out_specs=_row_spec(TM_FFN, D_MODEL),
        out_shape=jax.ShapeDtypeStruct((rows, D_MODEL), F32),
        compiler_params=_params(),
        name="ffn",
    )(*hs, mod, g, w13, w2, fg)


def _head_norm_rope(x, gmat, g, cos, sin, first_half):
    ms = _dot((x * x).astype(BF16), gmat)
    y = x * lax.rsqrt(ms + EPS) * g
    up = pltpu.roll(y, LANES - HEAD_DIM // 4, 1)
    dn = pltpu.roll(y, HEAD_DIM // 4, 1)
    return y * cos + jnp.where(first_half, up, dn) * sin


def _proj_kernel(*refs, kv_only):
    if kv_only:
        h_ref, mod_ref, g_ref, w_ref, kg_ref, cos_ref, sin_ref, gmat_ref, k_ref, vt_ref = refs
    else:
        (h_ref, mod_ref, g_ref, w_ref, qg_ref, kg_ref, cos_ref, sin_ref, gmat_ref,
         qt_ref, k_ref, vt_ref, pool_ref, sgu_ref) = refs
    lane = lax.broadcasted_iota(jnp.int32, (1, LANES), 1)
    first_half = (lane % (HEAD_DIM // 2)) < (HEAD_DIM // 4)
    base = 0 if kv_only else OFF_K

    def norm(j, _):
        return _modulate(h_ref[pl.ds(j * SUB, SUB), :], g_ref[...], mod_ref, 3).astype(BF16)

    def project(j, hn):
        return _dot(hn, w_ref[...])

    def finish(j, p):
        rows = pl.ds(j * SUB, SUB)
        gmat, cos, sin = gmat_ref[...], cos_ref[rows, :], sin_ref[rows, :]
        if not kv_only:
            for c in range(ATTN_WIDTH // LANES):
                qc = _head_norm_rope(p[:, c * LANES:(c + 1) * LANES], gmat, qg_ref[...], cos, sin, first_half)
                qt_ref[c * LANES:(c + 1) * LANES, rows] = (qc * Q_SCALE).T.astype(BF16)
            pool_ref[rows, :] = p[:, OFF_POOL:OFF_POOL + POOL_WIDTH]
            sgu_ref[rows, :] = p[:, OFF_POOL + POOL_WIDTH:OFF_GATE]
        k = _head_norm_rope(p[:, base:base + KV_WIDTH], gmat, kg_ref[...], cos, sin, first_half)
        k_ref[rows, :] = k.astype(BF16)
        vt = p[:, base + KV_WIDTH:base + 2 * KV_WIDTH].T.astype(BF16)
        ones = jnp.ones((V_ROWS - HEAD_DIM, SUB), BF16)
        for g in range(N_KV_HEADS):
            vt_ref[g * V_ROWS:g * V_ROWS + HEAD_DIM, rows] = vt[g * HEAD_DIM:(g + 1) * HEAD_DIM, :]
            vt_ref[g * V_ROWS + HEAD_DIM:(g + 1) * V_ROWS, rows] = ones

    _software_pipeline([norm, project, finish], TM_PROJ // SUB)


def _proj(h, mod, g, w, qg, kg, cos, sin, gmat, *, first_tile, n_tiles, kv_only):
    tm = TM_PROJ
    lat_tiles = LAT_ROWS // tm
    per_batch = SEQ // tm
    rows = n_tiles * tm
    h_spec = pl.BlockSpec((tm, D_MODEL), lambda r: (r + first_tile, 0))
    mod_spec = pl.BlockSpec(
        (None, N_MOD, D_MODEL),
        lambda r: (jnp.where(r + first_tile >= lat_tiles, BATCH, (r + first_tile) // per_batch), 0, 0))
    tab_spec = pl.BlockSpec(
        (tm, LANES), lambda r: (jnp.where(r + first_tile >= lat_tiles, per_batch, (r + first_tile) % per_batch), 0))
    gains = [_const_spec((1, LANES))] * (1 if kv_only else 2)
    in_specs = [h_spec, mod_spec, _const_spec((1, D_MODEL)), _const_spec(w.shape)] + gains + [
        tab_spec, tab_spec, _const_spec((LANES, LANES))]
    out_specs = [_row_spec(tm, KV_WIDTH), _col_spec(VT_ROWS, tm)]
    out_shape = [jax.ShapeDtypeStruct((rows, KV_WIDTH), BF16), jax.ShapeDtypeStruct((VT_ROWS, rows), BF16)]
    if not kv_only:
        out_specs = [_col_spec(ATTN_WIDTH, tm)] + out_specs + [_row_spec(tm, POOL_WIDTH), _row_spec(tm, 2 * SGU_WIDTH)]
        out_shape = ([jax.ShapeDtypeStruct((ATTN_WIDTH, rows), BF16)] + out_shape
                     + [jax.ShapeDtypeStruct((rows, POOL_WIDTH), F32),
                        jax.ShapeDtypeStruct((rows, 2 * SGU_WIDTH), F32)])
    args = (h, mod, g, w) + ((kg,) if kv_only else (qg, kg)) + (cos, sin, gmat)
    return pl.pallas_call(
        functools.partial(_proj_kernel, kv_only=kv_only),
        grid=(n_tiles,),
        in_specs=in_specs,
        out_specs=out_specs,
        out_shape=out_shape,
        compiler_params=_params(),
        name="proj_kv" if kv_only else "proj",
    )(*args)


def _attend(qt_ref, k_refs, vt_refs, o_ref):
    zeros = jnp.zeros((HEAD_DIM, TQ), BF16)
    w = []
    for h in range(N_HEADS):
        qh = qt_ref[h * HEAD_DIM:(h + 1) * HEAD_DIM, :]
        w.append(jnp.concatenate([qh, zeros] if h < GROUP_HEADS else [zeros, qh], axis=0))

    chunks = [(k_ref, vt_ref, c * KEY_CHUNK, min(KEY_CHUNK, k_ref.shape[0] - c * KEY_CHUNK))
              for k_ref, vt_ref in zip(k_refs, vt_refs)
              for c in range(pl.cdiv(k_ref.shape[0], KEY_CHUNK))]

    def scores(chunk, h):
        k_ref, _, start, size = chunk
        return _dot(k_ref[start:start + size, :], w[h])

    m = [None] * N_HEADS
    acc = [None] * N_HEADS
    s_next = [scores(chunks[0], h) for h in range(N_HEADS)]
    for c, (k_ref, vt_ref, start, size) in enumerate(chunks):
        s = s_next
        s_next = [None] * N_HEADS
        for h in range(N_HEADS):
            if c + 1 < len(chunks):
                s_next[h] = scores(chunks[c + 1], h)
            m_chunk = jnp.max(s[h], axis=0, keepdims=True)
            m_new = m_chunk if c == 0 else jnp.maximum(m[h], m_chunk)
            p = jnp.exp2(s[h] - m_new).astype(BF16)
            base = (h // GROUP_HEADS) * V_ROWS
            pv = _dot(vt_ref[base:base + V_ROWS, start:start + size], p)
            acc[h] = pv if c == 0 else acc[h] * jnp.exp2(m[h] - m_new) + pv
            m[h] = m_new
    outs = [a[:HEAD_DIM, :] / a[HEAD_DIM:HEAD_DIM + 1, :] for a in acc]
    o_ref[...] = jnp.concatenate(outs, axis=0).T.astype(BF16)


def _attn_kernel(qt_ref, kl_ref, kc_ref, vtl_ref, vtc_ref, o_ref, *, with_ctx_queries):
    if with_ctx_queries:
        i = pl.program_id(1)

        @pl.when(i < SEQ // TQ)
        def _():
            _attend(qt_ref, [kl_ref, kc_ref], [vtl_ref, vtc_ref], o_ref)

        @pl.when(i == SEQ // TQ)
        def _():
            _attend(qt_ref, [kc_ref], [vtc_ref], o_ref)
    else:
        _attend(qt_ref, [kl_ref, kc_ref], [vtl_ref, vtc_ref], o_ref)


def _attention(qt, k_lat, vt_lat, k_ctx, vt_ctx, *, ctx_block, with_ctx_queries):
    per_batch = SEQ // TQ
    n_q = per_batch + (1 if with_ctx_queries else 0)
    rows = ALL_ROWS if with_ctx_queries else LAT_ROWS

    def q_block(b, i):
        return jnp.where(i < per_batch, b * per_batch + i, LAT_ROWS // TQ + b)

    return pl.pallas_call(
        functools.partial(_attn_kernel, with_ctx_queries=with_ctx_queries),
        grid=(BATCH, n_q),
        in_specs=[
            pl.BlockSpec((ATTN_WIDTH, TQ), lambda b, i: (0, q_block(b, i))),
            pl.BlockSpec((SEQ, KV_WIDTH), lambda b, i: (b, 0)),
            pl.BlockSpec((CTX_LEN, KV_WIDTH), lambda b, i: (ctx_block + b, 0)),
            pl.BlockSpec((VT_ROWS, SEQ), lambda b, i: (0, b)),
            pl.BlockSpec((VT_ROWS, CTX_LEN), lambda b, i: (0, ctx_block + b)),
        ],
        out_specs=pl.BlockSpec((TQ, ATTN_WIDTH), lambda b, i: (q_block(b, i), 0)),
        out_shape=jax.ShapeDtypeStruct((rows, ATTN_WIDTH), BF16),
        compiler_params=_params(),
        name="attention",
    )(qt, k_lat, k_ctx, vt_lat, vt_ctx)


def _pool_branch(j, pool_ref, prev_ref, next_ref, xe_ref, q2_ref, q4_ref, q8_ref, wbd_ref, scale_ref):
    tm = SUB
    n_sub = TM_MERGE // SUB
    sub = pl.program_id(0) * n_sub + j
    is_ctx = sub >= LAT_ROWS // SUB
    per_seg = jnp.where(is_ctx, CTX_LEN // SUB, SEQ // SUB)
    seg_idx = jnp.where(is_ctx, sub - LAT_ROWS // SUB, sub) % per_seg
    first = seg_idx == 0
    last = seg_idx == per_seg - 1
    x = pool_ref[pl.ds(j * SUB, SUB), :]
    prev_rows = prev_ref[...] if j == 0 else pool_ref[pl.ds(j * SUB - POOL_HALO, POOL_HALO), :]
    next_rows = next_ref[...] if j == n_sub - 1 else pool_ref[pl.ds((j + 1) * SUB, POOL_HALO), :]
    xe_ref[0:POOL_HALO, :] = jnp.where(first, 0.0, prev_rows)
    xe_ref[POOL_HALO:POOL_HALO + tm, :] = x
    xe_ref[POOL_HALO + tm:2 * POOL_HALO + tm, :] = jnp.where(last, 0.0, next_rows)
    xe_ref[2 * POOL_HALO + tm:, :] = jnp.zeros((2 * POOL_HALO, POOL_WIDTH), F32)
    q2_ref[...] = xe_ref[0:tm + 24, :] + xe_ref[1:tm + 25, :]
    q4_ref[...] = q2_ref[0:tm + 16, :] + q2_ref[2:tm + 18, :]
    q8_ref[...] = q4_ref[0:tm + 8, LANES:] + q4_ref[4:tm + 12, LANES:]
    q16 = q8_ref[0:tm, :] + q8_ref[8:tm + 8, :]
    lane = lax.broadcasted_iota(jnp.int32, (1, LANES), 1)
    low = lane < POOL_GROUP_DIM
    sums = jnp.concatenate([
        jnp.where(low, q2_ref[7:7 + tm, 0:LANES], q4_ref[6:6 + tm, 0:LANES]),
        jnp.where(low, q8_ref[4:4 + tm, :], q16),
    ], axis=1)
    lane_w = lax.broadcasted_iota(jnp.int32, (1, POOL_WIDTH), 1)
    half = jnp.where(lane_w < 64, 1, jnp.where(lane_w < 128, 2, jnp.where(lane_w < 192, 4, 8)))
    seg_len = per_seg * SUB
    pos = seg_idx * SUB + lax.broadcasted_iota(jnp.int32, (tm, 1), 0)
    cnt = jnp.minimum(pos + half, seg_len) - jnp.maximum(pos - half, 0)
    pooled = sums / cnt.astype(F32) - x
    return _dot(pooled.astype(BF16), wbd_ref[...]) * scale_ref[...]


def _gelu_tanh(x):
    return x * (0.5 * (1.0 + jnp.tanh(0.7978845608028654 * (x + 0.044715 * (x * x * x)))))


def _sgu_branch(j, sgu_ref, norm_ref, wcat_ref, bias_ref):
    gl = _gelu_tanh(sgu_ref[pl.ds(j * SUB, SUB), :])
    u = gl[:, :SGU_WIDTH]
    v = _rms(gl[:, SGU_WIDTH:], norm_ref[...])
    lane = lax.broadcasted_iota(jnp.int32, (1, SGU_WIDTH), 1)
    outs = []
    for c in range(SUB // CHUNK):
        vc = v[c * CHUNK:(c + 1) * CHUNK, :]
        stack = jnp.concatenate(
            [jnp.where((lane >= g * SGU_GROUP_DIM) & (lane < (g + 1) * SGU_GROUP_DIM), vc, 0.0)
             for g in range(SGU_GROUPS)], axis=0).astype(BF16)
        outs.append(_dot(wcat_ref[...], stack) + bias_ref[...])
    return u * jnp.concatenate(outs, axis=0)


def _merge_kernel(h_ref, mod_ref, g_ref, attn_ref, pool_ref, prev_ref, next_ref, sgu_ref,
                  wgate_ref, wbd_ref, pscale_ref, snorm_ref, wcat_ref, sbias_ref,
                  wbp_ref, wba_ref, wbs_ref, wout_ref, o_ref,
                  xe_ref, q2_ref, q4_ref, q8_ref):
    def norm(j, _):
        return _modulate(h_ref[pl.ds(j * SUB, SUB), :], g_ref[...], mod_ref, 3).astype(BF16)

    def gates_and_mixers(j, hn):
        gates = [_dot(hn, wgate_ref[:, i * D_MODEL:(i + 1) * D_MODEL]) for i in range(3)]
        pool_out = _pool_branch(j, pool_ref, prev_ref, next_ref, xe_ref, q2_ref, q4_ref, q8_ref,
                                wbd_ref, pscale_ref)
        sgu_out = _sgu_branch(j, sgu_ref, snorm_ref, wcat_ref, sbias_ref)
        return gates, pool_out.astype(BF16), sgu_out.astype(BF16)

    def branches(j, carry):
        gates, pool_out, sgu_out = carry
        merged = jax.nn.sigmoid(gates[0]) * _dot(pool_out, wbp_ref[...])
        merged += jax.nn.sigmoid(gates[1]) * _dot(attn_ref[pl.ds(j * SUB, SUB), :], wba_ref[...])
        merged += jax.nn.sigmoid(gates[2]) * _dot(sgu_out, wbs_ref[...])
        return merged.astype(BF16)

    def store(j, merged):
        rows = pl.ds(j * SUB, SUB)
        o_ref[rows, :] = h_ref[rows, :] + mod_ref[5:6, :] * _dot(merged, wout_ref[...])

    _software_pipeline([norm, gates_and_mixers, branches, store], TM_MERGE // SUB)


def _merge(h, mod, g, attn, pool_in, sgu_in, wgate, wbd, pscale, snorm, wcat, sbias, wbp, wba, wbs, wout,
           *, layer, rows):
    tm = TM_MERGE
    blocks_per_tile = tm // POOL_HALO
    last_block = rows // POOL_HALO - 1
    prev_spec = pl.BlockSpec((POOL_HALO, POOL_WIDTH), lambda r: (jnp.maximum(r * blocks_per_tile - 1, 0), 0))
    next_spec = pl.BlockSpec((POOL_HALO, POOL_WIDTH),
                             lambda r: (jnp.minimum((r + 1) * blocks_per_tile, last_block), 0))
    return pl.pallas_call(
        _merge_kernel,
        grid=(rows // tm,),
        in_specs=[
            _row_spec(tm, D_MODEL),
            _mod_spec(tm),
            _const_spec((1, D_MODEL)),
            _row_spec(tm, ATTN_WIDTH),
            _row_spec(tm, POOL_WIDTH),
            prev_spec,
            next_spec,
            _row_spec(tm, 2 * SGU_WIDTH),
            _const_spec((D_MODEL, 3 * D_MODEL)),
            _const_spec((POOL_WIDTH, POOL_WIDTH)),
            _const_spec((1, POOL_WIDTH)),
            _const_spec((1, SGU_WIDTH)),
            _const_spec((CHUNK, SGU_GROUPS * CHUNK)),
            _const_spec((CHUNK, SGU_WIDTH)),
            _layer_spec((POOL_WIDTH, D_MODEL), layer),
            _layer_spec((ATTN_WIDTH, D_MODEL), layer),
            _layer_spec((SGU_WIDTH, D_MODEL), layer),
            _layer_spec((D_MODEL, D_MODEL), layer),
        ],
        out_specs=_row_spec(tm, D_MODEL),
        out_shape=jax.ShapeDtypeStruct((rows, D_MODEL), F32),
        scratch_shapes=[
            pltpu.VMEM((SUB + 32, POOL_WIDTH), F32),
            pltpu.VMEM((SUB + 24, POOL_WIDTH), F32),
            pltpu.VMEM((SUB + 16, POOL_WIDTH), F32),
            pltpu.VMEM((SUB + 8, LANES), F32),
        ],
        compiler_params=_params(),
        name="merge",
    )(h, mod, g, attn, pool_in, pool_in, pool_in, sgu_in, wgate, wbd, pscale, snorm, wcat, sbias,
      wbp, wba, wbs, wout)


def _rope_tables():
    pos = jnp.arange(SEQ, dtype=jnp.int32)
    row = (pos // GRID_W).astype(F32)[:, None]
    col = (pos % GRID_W).astype(F32)[:, None]
    half = HEAD_DIM // 2
    inv_freq = ROPE_THETA ** (-jnp.arange(0, half, 2, dtype=F32) / half)[None, :]
    ang = jnp.concatenate([row * inv_freq, row * inv_freq, col * inv_freq, col * inv_freq], axis=1)
    sign = jnp.tile(jnp.concatenate([-jnp.ones((half // 2,), F32), jnp.ones((half // 2,), F32)]), 2)[None, :]
    cos = jnp.concatenate([jnp.cos(ang), jnp.ones((TM_PROJ, HEAD_DIM), F32)], axis=0)
    sin = jnp.concatenate([jnp.sin(ang) * sign, jnp.zeros((TM_PROJ, HEAD_DIM), F32)], axis=0)
    return jnp.tile(cos, (1, LANES // HEAD_DIM)), jnp.tile(sin, (1, LANES // HEAD_DIM))


def _block_diag(blocks):
    n, r, c = blocks.shape
    eye = jnp.eye(n, dtype=blocks.dtype)
    return (eye[:, None, :, None] * blocks[:, :, None, :]).reshape(n * r, n * c)


def kernel(x, c, ctx, c_ctx, w_ada, b_ada, norm_ffn1, ffn1_w13, ffn1_w2, norm_mix, w_in, q_norm, k_norm,
           pool_w, pool_scale, sgu_norm, sgu_w, sgu_b, w_br_pool, w_br_attn, w_br_sgu, w_out, norm_ffn2,
           ffn2_w13, ffn2_w2, final_norm):
    cond = jnp.concatenate([c, c_ctx[None, :], jnp.zeros((COND_ROWS - BATCH - 1, D_MODEL), F32)], axis=0)
    mods = _adaln(cond, w_ada, b_ada).reshape(DEPTH, COND_ROWS, N_MOD, D_MODEL)
    cos, sin = _rope_tables()
    gmat = _block_diag(jnp.full((LANES // HEAD_DIM, HEAD_DIM, HEAD_DIM), 1.0 / HEAD_DIM, F32)).astype(BF16)
    fg = final_norm.reshape(1, D_MODEL)
    lat_tiles = LAT_ROWS // TM_PROJ
    ctx_tiles = CTX_ROWS // TM_PROJ

    ffn1_w13_bf, ffn1_w2_bf = ffn1_w13.astype(BF16), ffn1_w2.astype(BF16)
    ffn2_w13_bf, ffn2_w2_bf = ffn2_w13.astype(BF16), ffn2_w2.astype(BF16)
    w_br_pool_bf, w_br_attn_bf = w_br_pool.astype(BF16), w_br_attn.astype(BF16)
    w_br_sgu_bf, w_out_bf = w_br_sgu.astype(BF16), w_out.astype(BF16)

    hs = (x.reshape(LAT_ROWS, D_MODEL), ctx.reshape(CTX_ROWS, D_MODEL))
    for l in range(DEPTH):
        last = l == DEPTH - 1
        mod = mods[l]
        g_mix = norm_mix[l].reshape(1, D_MODEL)
        qg = jnp.tile(q_norm[l], LANES // HEAD_DIM).reshape(1, LANES)
        kg = jnp.tile(k_norm[l], LANES // HEAD_DIM).reshape(1, LANES)
        h = _ffn(hs, mod, norm_ffn1[l].reshape(1, D_MODEL), ffn1_w13_bf, ffn1_w2_bf,
                 fg, layer=l, mod_base=0, rows=ALL_ROWS, final=False)
        w_proj = w_in[l, :, :OFF_GATE].astype(BF16)
        if last:
            qt, k, vt, pool_in, sgu_in = _proj(h, mod, g_mix, w_proj, qg, kg, cos, sin, gmat,
                                               first_tile=0, n_tiles=lat_tiles, kv_only=False)
            k_ctx, vt_ctx = _proj(h, mod, g_mix, w_in[l, :, OFF_K:OFF_POOL].astype(BF16), None, kg, cos, sin, gmat,
                                  first_tile=lat_tiles, n_tiles=ctx_tiles, kv_only=True)
            attn = _attention(qt, k, vt, k_ctx, vt_ctx, ctx_block=0, with_ctx_queries=False)
            rows = LAT_ROWS
        else:
            qt, k, vt, pool_in, sgu_in = _proj(h, mod, g_mix, w_proj, qg, kg, cos, sin, gmat,
                                               first_tile=0, n_tiles=lat_tiles + ctx_tiles, kv_only=False)
            attn = _attention(qt, k, vt, k, vt, ctx_block=LAT_ROWS // CTX_LEN, with_ctx_queries=True)
            rows = ALL_ROWS
        h = _merge(
            h, mod, g_mix, attn, pool_in, sgu_in, w_in[l, :, OFF_GATE:].astype(BF16),
            _block_diag(pool_w[l]).astype(BF16), pool_scale[l].reshape(1, POOL_WIDTH),
            sgu_norm[l].reshape(1, SGU_WIDTH),
            jnp.transpose(sgu_w[l], (1, 0, 2)).reshape(CHUNK, SGU_GROUPS * CHUNK).astype(BF16),
            jnp.repeat(sgu_b[l].T, SGU_GROUP_DIM, axis=1),
            w_br_pool_bf, w_br_attn_bf, w_br_sgu_bf, w_out_bf, layer=l, rows=rows)
        hs = (_ffn((h,), mod, norm_ffn2[l].reshape(1, D_MODEL), ffn2_w13_bf,
                   ffn2_w2_bf, fg, layer=l, mod_base=6, rows=rows, final=last),)
    return hs[0].reshape(BATCH, SEQ, D_MODEL)
```

```python
import functools

import jax
import jax.numpy as jnp
from jax import lax
from jax.experimental import pallas as pl
from jax.experimental.pallas import tpu as pltpu

D_MODEL = 1024
BATCH = 16
SEQ = 2048
DEPTH = 2
GRID_W = 64
CTX_LEN = 256
N_HEADS = 8
N_KV_HEADS = 2
HEAD_DIM = 64
ATTN_WIDTH = N_HEADS * HEAD_DIM
KV_WIDTH = N_KV_HEADS * HEAD_DIM
ROPE_THETA = 10000.0
POOL_WINDOWS = (2, 4, 8, 16)
POOL_GROUPS = 4
POOL_GROUP_DIM = 64
POOL_WIDTH = POOL_GROUPS * POOL_GROUP_DIM
SGU_GROUPS = 4
SGU_GROUP_DIM = 64
SGU_WIDTH = SGU_GROUPS * SGU_GROUP_DIM
CHUNK = 128
D_FF = 2816
EPS = 1e-6
N_MOD = 9
OFF_K = ATTN_WIDTH
OFF_POOL = ATTN_WIDTH + 2 * KV_WIDTH
OFF_GATE = OFF_POOL + POOL_WIDTH + 2 * SGU_WIDTH

LAT_ROWS = BATCH * SEQ
CTX_ROWS = BATCH * CTX_LEN
ALL_ROWS = LAT_ROWS + CTX_ROWS
GROUP_HEADS = N_HEADS // N_KV_HEADS
LANES = 128
COND_ROWS = 24
POOL_HALO = max(POOL_WINDOWS) // 2
BF16_ROWS = 16
V_ROWS = HEAD_DIM + BF16_ROWS
VT_ROWS = N_KV_HEADS * V_ROWS
Q_SCALE = HEAD_DIM ** -0.5 * 1.4426950408889634
SUB = 256
TM_FFN = 1024
TM_PROJ = 2048
TM_MERGE = 1024
TQ = CTX_LEN
KEY_CHUNK = 256
VMEM_LIMIT = 56 * 1024 * 1024

F32 = jnp.float32
BF16 = jnp.bfloat16


def _dot(a, b):
    return jnp.dot(a, b, preferred_element_type=F32)


def _rms(x, g):
    return x * lax.rsqrt(jnp.mean(x * x, axis=-1, keepdims=True) + EPS) * g


def _modulate(x, g, mod_ref, base):
    shift = mod_ref[base:base + 1, :]
    scale = mod_ref[base + 1:base + 2, :]
    return _rms(x, g) * (1.0 + scale) + shift


def _const_spec(shape):
    zeros = (0,) * len(shape)
    return pl.BlockSpec(shape, lambda *_: zeros, pipeline_mode=pl.Buffered(1))


def _layer_spec(shape, layer):
    index = (layer,) + (0,) * len(shape)
    return pl.BlockSpec((None,) + tuple(shape), lambda *_: index, pipeline_mode=pl.Buffered(1))


def _row_spec(tm, width):
    return pl.BlockSpec((tm, width), lambda r: (r, 0))


def _col_spec(height, tm):
    return pl.BlockSpec((height, tm), lambda r: (0, r))


def _mod_spec(tm):
    lat_tiles = LAT_ROWS // tm
    per_batch = SEQ // tm
    return pl.BlockSpec((None, N_MOD, D_MODEL),
                        lambda r: (jnp.where(r >= lat_tiles, BATCH, r // per_batch), 0, 0))


def _params():
    return pltpu.CompilerParams(vmem_limit_bytes=VMEM_LIMIT)


def _software_pipeline(stages, n_sub):
    carry = [None] * n_sub
    for step in range(n_sub + len(stages) - 1):
        for s in reversed(range(len(stages))):
            j = step - s
            if 0 <= j < n_sub:
                carry[j] = stages[s](j, carry[j])


def _adaln_kernel(cond_ref, w_ref, b_ref, o_ref):
    cond = cond_ref[...]
    s = (cond * jax.nn.sigmoid(cond)).astype(BF16)
    o_ref[...] = _dot(s, w_ref[...].astype(BF16)) + b_ref[...]


def _adaln(cond, w_ada, b_ada):
    return pl.pallas_call(
        _adaln_kernel,
        grid=(DEPTH, N_MOD),
        in_specs=[
            pl.BlockSpec((COND_ROWS, D_MODEL), lambda l, j: (0, 0)),
            pl.BlockSpec((None, D_MODEL, D_MODEL), lambda l, j: (l, 0, j)),
            pl.BlockSpec((None, 1, D_MODEL), lambda l, j: (l, 0, j)),
        ],
        out_specs=pl.BlockSpec((None, COND_ROWS, D_MODEL), lambda l, j: (l, 0, j)),
        out_shape=jax.ShapeDtypeStruct((DEPTH, COND_ROWS, N_MOD * D_MODEL), F32),
        compiler_params=_params(),
        name="adaln",
    )(cond, w_ada, b_ada.reshape(DEPTH, 1, N_MOD * D_MODEL))


def _ffn_kernel(*refs, mod_base, final, split_input):
    if split_input:
        lat_ref, ctx_ref, mod_ref, g_ref, w13_ref, w2_ref, fg_ref, o_ref = refs
        is_lat = pl.program_id(0) < LAT_ROWS // TM_FFN
    else:
        h_ref, mod_ref, g_ref, w13_ref, w2_ref, fg_ref, o_ref = refs

    def load(j):
        rows = pl.ds(j * SUB, SUB)
        if split_input:
            return jnp.where(is_lat, lat_ref[rows, :], ctx_ref[rows, :])
        return h_ref[rows, :]

    def norm(j, _):
        return _modulate(load(j), g_ref[...], mod_ref, mod_base).astype(BF16)

    def up(j, xn):
        return _dot(xn, w13_ref[:, :D_FF]), _dot(xn, w13_ref[:, D_FF:])

    def act(j, ab):
        a, b = ab
        return (a * jax.nn.sigmoid(a) * b).astype(BF16)

    def down(j, u):
        return _dot(u, w2_ref[...])

    def store(j, y):
        out = load(j) + (0.5 * mod_ref[mod_base + 2:mod_base + 3, :]) * y
        if final:
            out = _rms(out, fg_ref[...])
        o_ref[pl.ds(j * SUB, SUB), :] = out

    _software_pipeline([norm, up, act, down, store], TM_FFN // SUB)


def _ffn(hs, mod, g, w13, w2, fg, *, layer, mod_base, rows, final):
    split_input = len(hs) == 2
    lat_tiles = LAT_ROWS // TM_FFN
    if split_input:
        h_specs = [pl.BlockSpec((TM_FFN, D_MODEL), lambda r: (jnp.minimum(r, lat_tiles - 1), 0)),
                   pl.BlockSpec((TM_FFN, D_MODEL), lambda r: (jnp.maximum(r - lat_tiles, 0), 0))]
    else:
        h_specs = [_row_spec(TM_FFN, D_MODEL)]
    return pl.pallas_call(
        functools.partial(_ffn_kernel, mod_base=mod_base, final=final, split_input=split_input),
        grid=(rows // TM_FFN,),
        in_specs=h_specs + [
            _mod_spec(TM_FFN),
            _const_spec((1, D_MODEL)),
            _layer_spec((D_MODEL, 2 * D_FF), layer),
            _layer_spec((D_FF, D_MODEL), layer),
            _const_spec((1, D_MODEL)),
        ],
        out_specs=_row_spec(TM_FFN, D_MODEL),
        out_shape=jax.ShapeDtypeStruct((rows, D_MODEL), F32),
        compiler_params=_params(),
        name="ffn",
    )(*hs, mod, g, w13, w2, fg)


def _head_norm_rope(x, gmat, g, cos, sin, first_half):
    ms = _dot((x * x).astype(BF16), gmat)
    y = x * lax.rsqrt(ms + EPS) * g
    up = pltpu.roll(y, LANES - HEAD_DIM // 4, 1)
    dn = pltpu.roll(y, HEAD_DIM // 4, 1)
    return y * cos + jnp.where(first_half, up, dn) * sin


def _proj_kernel(*refs, kv_only):
    if kv_only:
        h_ref, mod_ref, g_ref, w_ref, kg_ref, cos_ref, sin_ref, gmat_ref, k_ref, vt_ref = refs
    else:
        (h_ref, mod_ref, g_ref, w_ref, qg_ref, kg_ref, cos_ref, sin_ref, gmat_ref,
         qt_ref, k_ref, vt_ref, pool_ref, sgu_ref) = refs
    lane = lax.broadcasted_iota(jnp.int32, (1, LANES), 1)
    first_half = (lane % (HEAD_DIM // 2)) < (HEAD_DIM // 4)
    base = 0 if kv_only else OFF_K

    def norm(j, _):
        return _modulate(h_ref[pl.ds(j * SUB, SUB), :], g_ref[...], mod_ref, 3).astype(BF16)

    def project(j, hn):
        return _dot(hn, w_ref[...])

    def finish(j, p):
        rows = pl.ds(j * SUB, SUB)
        gmat, cos, sin = gmat_ref[...], cos_ref[rows, :], sin_ref[rows, :]
        if not kv_only:
            for c in range(ATTN_WIDTH // LANES):
                qc = _head_norm_rope(p[:, c * LANES:(c + 1) * LANES], gmat, qg_ref[...], cos, sin, first_half)
                qt_ref[c * LANES:(c + 1) * LANES, rows] = (qc * Q_SCALE).T.astype(BF16)
            pool_ref[rows, :] = p[:, OFF_POOL:OFF_POOL + POOL_WIDTH]
            sgu_ref[rows, :] = p[:, OFF_POOL + POOL_WIDTH:OFF_GATE]
        k = _head_norm_rope(p[:, base:base + KV_WIDTH], gmat, kg_ref[...], cos, sin, first_half)
        k_ref[rows, :] = k.astype(BF16)
        vt = p[:, base + KV_WIDTH:base + 2 * KV_WIDTH].T.astype(BF16)
        ones = jnp.ones((V_ROWS - HEAD_DIM, SUB), BF16)
        for g in range(N_KV_HEADS):
            vt_ref[g * V_ROWS:g * V_ROWS + HEAD_DIM, rows] = vt[g * HEAD_DIM:(g + 1) * HEAD_DIM, :]
            vt_ref[g * V_ROWS + HEAD_DIM:(g + 1) * V_ROWS, rows] = ones

    _software_pipeline([norm, project, finish], TM_PROJ // SUB)


def _proj(h, mod, g, w, qg, kg, cos, sin, gmat, *, first_tile, n_tiles, kv_only):
    tm = TM_PROJ
    lat_tiles = LAT_ROWS // tm
    per_batch = SEQ // tm
    rows = n_tiles * tm
    h_spec = pl.BlockSpec((tm, D_MODEL), lambda r: (r + first_tile, 0))
    mod_spec = pl.BlockSpec(
        (None, N_MOD, D_MODEL),
        lambda r: (jnp.where(r + first_tile >= lat_tiles, BATCH, (r + first_tile) // per_batch), 0, 0))
    tab_spec = pl.BlockSpec(
        (tm, LANES), lambda r: (jnp.where(r + first_tile >= lat_tiles, per_batch, (r + first_tile) % per_batch), 0))
    gains = [_const_spec((1, LANES))] * (1 if kv_only else 2)
    in_specs = [h_spec, mod_spec, _const_spec((1, D_MODEL)), _const_spec(w.shape)] + gains + [
        tab_spec, tab_spec, _const_spec((LANES, LANES))]
    out_specs = [_row_spec(tm, KV_WIDTH), _col_spec(VT_ROWS, tm)]
    out_shape = [jax.ShapeDtypeStruct((rows, KV_WIDTH), BF16), jax.ShapeDtypeStruct((VT_ROWS, rows), BF16)]
    if not kv_only:
        out_specs = [_col_spec(ATTN_WIDTH, tm)] + out_specs + [_row_spec(tm, POOL_WIDTH), _row_spec(tm, 2 * SGU_WIDTH)]
        out_shape = ([jax.ShapeDtypeStruct((ATTN_WIDTH, rows), BF16)] + out_shape
                     + [jax.ShapeDtypeStruct((rows, POOL_WIDTH), F32),
                        jax.ShapeDtypeStruct((rows, 2 * SGU_WIDTH), F32)])
    args = (h, mod, g, w) + ((kg,) if kv_only else (qg, kg)) + (cos, sin, gmat)
    return pl.pallas_call(
        functools.partial(_proj_kernel, kv_only=kv_only),
        grid=(n_tiles,),
        in_specs=in_specs,
        out_specs=out_specs,
        out_shape=out_shape,
        compiler_params=_params(),
        name="proj_kv" if kv_only else "proj",
    )(*args)


def _attend(qt_ref, k_refs, vt_refs, o_ref):
    zeros = jnp.zeros((HEAD_DIM, TQ), BF16)
    w = []
    for h in range(N_HEADS):
        qh = qt_ref[h * HEAD_DIM:(h + 1) * HEAD_DIM, :]
        w.append(jnp.concatenate([qh, zeros] if h < GROUP_HEADS else [zeros, qh], axis=0))

    chunks = [(k_ref, vt_ref, c * KEY_CHUNK, min(KEY_CHUNK, k_ref.shape[0] - c * KEY_CHUNK))
              for k_ref, vt_ref in zip(k_refs, vt_refs)
              for c in range(pl.cdiv(k_ref.shape[0], KEY_CHUNK))]

    def scores(chunk, h):
        k_ref, _, start, size = chunk
        return _dot(k_ref[start:start + size, :], w[h])

    m = [None] * N_HEADS
    acc = [None] * N_HEADS
    s_next = [scores(chunks[0], h) for h in range(N_HEADS)]
    for c, (k_ref, vt_ref, start, size) in enumerate(chunks):
        s = s_next
        s_next = [None] * N_HEADS
        for h in range(N_HEADS):
            if c + 1 < len(chunks):
                s_next[h] = scores(chunks[c + 1], h)
            m_chunk = jnp.max(s[h], axis=0, keepdims=True)
            m_new = m_chunk if c == 0 else jnp.maximum(m[h], m_chunk)
            p = jnp.exp2(s[h] - m_new).astype(BF16)
            base = (h // GROUP_HEADS) * V_ROWS
            pv = _dot(vt_ref[base:base + V_ROWS, start:start + size], p)
            acc[h] = pv if c == 0 else acc[h] * jnp.exp2(m[h] - m_new) + pv
            m[h] = m_new
    outs = [a[:HEAD_DIM, :] / a[HEAD_DIM:HEAD_DIM + 1, :] for a in acc]
    o_ref[...] = jnp.concatenate(outs, axis=0).T.astype(BF16)


def _attn_kernel(qt_ref, kl_ref, kc_ref, vtl_ref, vtc_ref, o_ref, *, with_ctx_queries):
    if with_ctx_queries:
        i = pl.program_id(1)

        @pl.when(i < SEQ // TQ)
        def _():
            _attend(qt_ref, [kl_ref, kc_ref], [vtl_ref, vtc_ref], o_ref)

        @pl.when(i == SEQ // TQ)
        def _():
            _attend(qt_ref, [kc_ref], [vtc_ref], o_ref)
    else:
        _attend(qt_ref, [kl_ref, kc_ref], [vtl_ref, vtc_ref], o_ref)


def _attention(qt, k_lat, vt_lat, k_ctx, vt_ctx, *, ctx_block, with_ctx_queries):
    per_batch = SEQ // TQ
    n_q = per_batch + (1 if with_ctx_queries else 0)
    rows = ALL_ROWS if with_ctx_queries else LAT_ROWS

    def q_block(b, i):
        return jnp.where(i < per_batch, b * per_batch + i, LAT_ROWS // TQ + b)

    return pl.pallas_call(
        functools.partial(_attn_kernel, with_ctx_queries=with_ctx_queries),
        grid=(BATCH, n_q),
        in_specs=[
            pl.BlockSpec((ATTN_WIDTH, TQ), lambda b, i: (0, q_block(b, i))),
            pl.BlockSpec((SEQ, KV_WIDTH), lambda b, i: (b, 0)),
            pl.BlockSpec((CTX_LEN, KV_WIDTH), lambda b, i: (ctx_block + b, 0)),
            pl.BlockSpec((VT_ROWS, SEQ), lambda b, i: (0, b)),
            pl.BlockSpec((VT_ROWS, CTX_LEN), lambda b, i: (0, ctx_block + b)),
        ],
        out_specs=pl.BlockSpec((TQ, ATTN_WIDTH), lambda b, i: (q_block(b, i), 0)),
        out_shape=jax.ShapeDtypeStruct((rows, ATTN_WIDTH), BF16),
        compiler_params=_params(),
        name="attention",
    )(qt, k_lat, k_ctx, vt_lat, vt_ctx)


def _pool_branch(j, pool_ref, prev_ref, next_ref, inv_ref, xe_ref, q2_ref, q4_ref, q8_ref):
    tm = SUB
    n_sub = TM_MERGE // SUB
    sub = pl.program_id(0) * n_sub + j
    is_ctx = sub >= LAT_ROWS // SUB
    per_seg = jnp.where(is_ctx, CTX_LEN // SUB, SEQ // SUB)
    seg_idx = jnp.where(is_ctx, sub - LAT_ROWS // SUB, sub) % per_seg
    first = seg_idx == 0
    last = seg_idx == per_seg - 1
    x = pool_ref[pl.ds(j * SUB, SUB), :]
    prev_rows = prev_ref[...] if j == 0 else pool_ref[pl.ds(j * SUB - POOL_HALO, POOL_HALO), :]
    next_rows = next_ref[...] if j == n_sub - 1 else pool_ref[pl.ds((j + 1) * SUB, POOL_HALO), :]
    xe_ref[0:POOL_HALO, :] = jnp.where(first, 0.0, prev_rows)
    xe_ref[POOL_HALO:POOL_HALO + tm, :] = x
    xe_ref[POOL_HALO + tm:2 * POOL_HALO + tm, :] = jnp.where(last, 0.0, next_rows)
    xe_ref[2 * POOL_HALO + tm:, :] = jnp.zeros((2 * POOL_HALO, POOL_WIDTH), F32)
    q2_ref[...] = xe_ref[0:tm + 24, :] + xe_ref[1:tm + 25, :]
    q4_ref[...] = q2_ref[0:tm + 16, :] + q2_ref[2:tm + 18, :]
    q8_ref[...] = q4_ref[0:tm + 8, LANES:] + q4_ref[4:tm + 12, LANES:]
    q16 = q8_ref[0:tm, :] + q8_ref[8:tm + 8, :]
    lane = lax.broadcasted_iota(jnp.int32, (1, LANES), 1)
    low = lane < POOL_GROUP_DIM
    sums = jnp.concatenate([
        jnp.where(low, q2_ref[7:7 + tm, 0:LANES], q4_ref[6:6 + tm, 0:LANES]),
        jnp.where(low, q8_ref[4:4 + tm, :], q16),
    ], axis=1)
    return (sums * inv_ref[pl.ds(j * SUB, SUB), :] - x).astype(BF16)


def _gelu_tanh(x):
    return x * (0.5 * (1.0 + jnp.tanh(0.7978845608028654 * (x + 0.044715 * (x * x * x)))))


def _sgu_prepare(j, sgu_ref, norm_ref):
    gl = _gelu_tanh(sgu_ref[pl.ds(j * SUB, SUB), :])
    u = gl[:, :SGU_WIDTH]
    v = _rms(gl[:, SGU_WIDTH:], norm_ref[...])
    lane = lax.broadcasted_iota(jnp.int32, (1, SGU_WIDTH), 1)
    stacks = []
    for c in range(SUB // CHUNK):
        vc = v[c * CHUNK:(c + 1) * CHUNK, :]
        stacks.append(jnp.concatenate(
            [jnp.where((lane >= g * SGU_GROUP_DIM) & (lane < (g + 1) * SGU_GROUP_DIM), vc, 0.0)
             for g in range(SGU_GROUPS)], axis=0).astype(BF16))
    return u, stacks


def _merge_kernel(h_ref, mod_ref, g_ref, attn_ref, pool_ref, prev_ref, next_ref, inv_ref, sgu_ref,
                  wgate_ref, wbd_ref, pscale_ref, snorm_ref, wcat_ref, sbias_ref,
                  wbp_ref, wba_ref, wbs_ref, wout_ref, o_ref,
                  xe_ref, q2_ref, q4_ref, q8_ref):
    def prepare(j, _):
        hn = _modulate(h_ref[pl.ds(j * SUB, SUB), :], g_ref[...], mod_ref, 3).astype(BF16)
        pooled = _pool_branch(j, pool_ref, prev_ref, next_ref, inv_ref, xe_ref, q2_ref, q4_ref, q8_ref)
        u, stacks = _sgu_prepare(j, sgu_ref, snorm_ref)
        return hn, pooled, u, stacks

    def gates(j, carry):
        hn, pooled, u, stacks = carry
        return [_dot(hn, wgate_ref[:, i * D_MODEL:(i + 1) * D_MODEL]) for i in range(3)], pooled, u, stacks

    def branches(j, carry):
        gate_logits, pooled, u, stacks = carry
        pool_out = (_dot(pooled, wbd_ref[...]) * pscale_ref[...]).astype(BF16)
        spatial = jnp.concatenate([_dot(wcat_ref[...], st) + sbias_ref[...] for st in stacks], axis=0)
        sgu_out = (u * spatial).astype(BF16)
        merged = jax.nn.sigmoid(gate_logits[0]) * _dot(pool_out, wbp_ref[...])
        merged += jax.nn.sigmoid(gate_logits[1]) * _dot(attn_ref[pl.ds(j * SUB, SUB), :], wba_ref[...])
        merged += jax.nn.sigmoid(gate_logits[2]) * _dot(sgu_out, wbs_ref[...])
        return merged.astype(BF16)

    def store(j, merged):
        rows = pl.ds(j * SUB, SUB)
        o_ref[rows, :] = h_ref[rows, :] + mod_ref[5:6, :] * _dot(merged, wout_ref[...])

    _software_pipeline([prepare, gates, branches, store], TM_MERGE // SUB)


def _merge(h, mod, g, attn, pool_in, pool_inv, sgu_in, wgate, wbd, pscale, snorm, wcat, sbias, wbp, wba, wbs, wout,
           *, layer, rows):
    tm = TM_MERGE
    blocks_per_tile = tm // POOL_HALO
    last_block = rows // POOL_HALO - 1
    prev_spec = pl.BlockSpec((POOL_HALO, POOL_WIDTH), lambda r: (jnp.maximum(r * blocks_per_tile - 1, 0), 0))
    next_spec = pl.BlockSpec((POOL_HALO, POOL_WIDTH),
                             lambda r: (jnp.minimum((r + 1) * blocks_per_tile, last_block), 0))
    lat_tiles = LAT_ROWS // tm
    per_batch = SEQ // tm
    inv_spec = pl.BlockSpec((tm, POOL_WIDTH), lambda r: (jnp.where(r >= lat_tiles, per_batch, r % per_batch), 0))
    return pl.pallas_call(
        _merge_kernel,
        grid=(rows // tm,),
        in_specs=[
            _row_spec(tm, D_MODEL),
            _mod_spec(tm),
            _const_spec((1, D_MODEL)),
            _row_spec(tm, ATTN_WIDTH),
            _row_spec(tm, POOL_WIDTH),
            prev_spec,
            next_spec,
            inv_spec,
            _row_spec(tm, 2 * SGU_WIDTH),
            _const_spec((D_MODEL, 3 * D_MODEL)),
            _const_spec((POOL_WIDTH, POOL_WIDTH)),
            _const_spec((1, POOL_WIDTH)),
            _const_spec((1, SGU_WIDTH)),
            _const_spec((CHUNK, SGU_GROUPS * CHUNK)),
            _const_spec((CHUNK, SGU_WIDTH)),
            _layer_spec((POOL_WIDTH, D_MODEL), layer),
            _layer_spec((ATTN_WIDTH, D_MODEL), layer),
            _layer_spec((SGU_WIDTH, D_MODEL), layer),
            _layer_spec((D_MODEL, D_MODEL), layer),
        ],
        out_specs=_row_spec(tm, D_MODEL),
        out_shape=jax.ShapeDtypeStruct((rows, D_MODEL), F32),
        scratch_shapes=[
            pltpu.VMEM((SUB + 32, POOL_WIDTH), F32),
            pltpu.VMEM((SUB + 24, POOL_WIDTH), F32),
            pltpu.VMEM((SUB + 16, POOL_WIDTH), F32),
            pltpu.VMEM((SUB + 8, LANES), F32),
        ],
        compiler_params=_params(),
        name="merge",
    )(h, mod, g, attn, pool_in, pool_in, pool_in, pool_inv, sgu_in, wgate, wbd, pscale, snorm, wcat, sbias,
      wbp, wba, wbs, wout)


def _rope_tables():
    pos = jnp.arange(SEQ, dtype=jnp.int32)
    row = (pos // GRID_W).astype(F32)[:, None]
    col = (pos % GRID_W).astype(F32)[:, None]
    half = HEAD_DIM // 2
    inv_freq = ROPE_THETA ** (-jnp.arange(0, half, 2, dtype=F32) / half)[None, :]
    ang = jnp.concatenate([row * inv_freq, row * inv_freq, col * inv_freq, col * inv_freq], axis=1)
    sign = jnp.tile(jnp.concatenate([-jnp.ones((half // 2,), F32), jnp.ones((half // 2,), F32)]), 2)[None, :]
    cos = jnp.concatenate([jnp.cos(ang), jnp.ones((TM_PROJ, HEAD_DIM), F32)], axis=0)
    sin = jnp.concatenate([jnp.sin(ang) * sign, jnp.zeros((TM_PROJ, HEAD_DIM), F32)], axis=0)
    return jnp.tile(cos, (1, LANES // HEAD_DIM)), jnp.tile(sin, (1, LANES // HEAD_DIM))


def _pool_inverse_counts():
    half = jnp.repeat(jnp.array([w // 2 for w in POOL_WINDOWS], jnp.int32), POOL_GROUP_DIM)[None, :]

    def table(seg_len):
        pos = jnp.arange(seg_len, dtype=jnp.int32)[:, None]
        cnt = jnp.minimum(pos + half, seg_len) - jnp.maximum(pos - half, 0)
        return 1.0 / cnt.astype(F32)

    return jnp.concatenate([table(SEQ), jnp.tile(table(CTX_LEN), (TM_MERGE // CTX_LEN, 1))], axis=0)


def _block_diag(blocks):
    n, r, c = blocks.shape
    eye = jnp.eye(n, dtype=blocks.dtype)
    return (eye[:, None, :, None] * blocks[:, :, None, :]).reshape(n * r, n * c)


def kernel(x, c, ctx, c_ctx, w_ada, b_ada, norm_ffn1, ffn1_w13, ffn1_w2, norm_mix, w_in, q_norm, k_norm,
           pool_w, pool_scale, sgu_norm, sgu_w, sgu_b, w_br_pool, w_br_attn, w_br_sgu, w_out, norm_ffn2,
           ffn2_w13, ffn2_w2, final_norm):
    cond = jnp.concatenate([c, c_ctx[None, :], jnp.zeros((COND_ROWS - BATCH - 1, D_MODEL), F32)], axis=0)
    mods = _adaln(cond, w_ada, b_ada).reshape(DEPTH, COND_ROWS, N_MOD, D_MODEL)
    cos, sin = _rope_tables()
    gmat = _block_diag(jnp.full((LANES // HEAD_DIM, HEAD_DIM, HEAD_DIM), 1.0 / HEAD_DIM, F32)).astype(BF16)
    fg = final_norm.reshape(1, D_MODEL)
    pool_inv = _pool_inverse_counts()
    lat_tiles = LAT_ROWS // TM_PROJ
    ctx_tiles = CTX_ROWS // TM_PROJ

    ffn1_w13_bf, ffn1_w2_bf = ffn1_w13.astype(BF16), ffn1_w2.astype(BF16)
    ffn2_w13_bf, ffn2_w2_bf = ffn2_w13.astype(BF16), ffn2_w2.astype(BF16)
    w_br_pool_bf, w_br_attn_bf = w_br_pool.astype(BF16), w_br_attn.astype(BF16)
    w_br_sgu_bf, w_out_bf = w_br_sgu.astype(BF16), w_out.astype(BF16)

    hs = (x.reshape(LAT_ROWS, D_MODEL), ctx.reshape(CTX_ROWS, D_MODEL))
    for l in range(DEPTH):
        last = l == DEPTH - 1
        mod = mods[l]
        g_mix = norm_mix[l].reshape(1, D_MODEL)
        qg = jnp.tile(q_norm[l], LANES // HEAD_DIM).reshape(1, LANES)
        kg = jnp.tile(k_norm[l], LANES // HEAD_DIM).reshape(1, LANES)
        h = _ffn(hs, mod, norm_ffn1[l].reshape(1, D_MODEL), ffn1_w13_bf, ffn1_w2_bf,
                 fg, layer=l, mod_base=0, rows=ALL_ROWS, final=False)
        w_proj = w_in[l, :, :OFF_GATE].astype(BF16)
        if last:
            qt, k, vt, pool_in, sgu_in = _proj(h, mod, g_mix, w_proj, qg, kg, cos, sin, gmat,
                                               first_tile=0, n_tiles=lat_tiles, kv_only=False)
            k_ctx, vt_ctx = _proj(h, mod, g_mix, w_in[l, :, OFF_K:OFF_POOL].astype(BF16), None, kg, cos, sin, gmat,
                                  first_tile=lat_tiles, n_tiles=ctx_tiles, kv_only=True)
            attn = _attention(qt, k, vt, k_ctx, vt_ctx, ctx_block=0, with_ctx_queries=False)
            rows = LAT_ROWS
        else:
            qt, k, vt, pool_in, sgu_in = _proj(h, mod, g_mix, w_proj, qg, kg, cos, sin, gmat,
                                               first_tile=0, n_tiles=lat_tiles + ctx_tiles, kv_only=False)
            attn = _attention(qt, k, vt, k, vt, ctx_block=LAT_ROWS // CTX_LEN, with_ctx_queries=True)
            rows = ALL_ROWS
        h = _merge(
            h, mod, g_mix, attn, pool_in, pool_inv, sgu_in, w_in[l, :, OFF_GATE:].astype(BF16),
            _block_diag(pool_w[l]).astype(BF16), pool_scale[l].reshape(1, POOL_WIDTH),
            sgu_norm[l].reshape(1, SGU_WIDTH),
            jnp.transpose(sgu_w[l], (1, 0, 2)).reshape(CHUNK, SGU_GROUPS * CHUNK).astype(BF16),
            jnp.repeat(sgu_b[l].T, SGU_GROUP_DIM, axis=1),
            w_br_pool_bf, w_br_attn_bf, w_br_sgu_bf, w_out_bf, layer=l, rows=rows)
        hs = (_ffn((h,), mod, norm_ffn2[l].reshape(1, D_MODEL), ffn2_w13_bf,
                   ffn2_w2_bf, fg, layer=l, mod_base=6, rows=rows, final=last),)
    return hs[0].reshape(BATCH, SEQ, D_MODEL)
```

```python
import functools

import jax
import jax.numpy as jnp
from jax import lax
from jax.experimental import pallas as pl
from jax.experimental.pallas import tpu as pltpu

D_MODEL = 1024
BATCH = 16
SEQ = 2048
DEPTH = 2
GRID_W = 64
CTX_LEN = 256
N_HEADS = 8
N_KV_HEADS = 2
HEAD_DIM = 64
ATTN_WIDTH = N_HEADS * HEAD_DIM
KV_WIDTH = N_KV_HEADS * HEAD_DIM
ROPE_THETA = 10000.0
POOL_WINDOWS = (2, 4, 8, 16)
POOL_GROUPS = 4
POOL_GROUP_DIM = 64
POOL_WIDTH = POOL_GROUPS * POOL_GROUP_DIM
SGU_GROUPS = 4
SGU_GROUP_DIM = 64
SGU_WIDTH = SGU_GROUPS * SGU_GROUP_DIM
CHUNK = 128
D_FF = 2816
EPS = 1e-6
N_MOD = 9
OFF_K = ATTN_WIDTH
OFF_POOL = ATTN_WIDTH + 2 * KV_WIDTH
OFF_GATE = OFF_POOL + POOL_WIDTH + 2 * SGU_WIDTH

LAT_ROWS = BATCH * SEQ
CTX_ROWS = BATCH * CTX_LEN
ALL_ROWS = LAT_ROWS + CTX_ROWS
GROUP_HEADS = N_HEADS // N_KV_HEADS
LANES = 128
COND_ROWS = 24
POOL_HALO = max(POOL_WINDOWS) // 2
BF16_ROWS = 16
V_ROWS = HEAD_DIM + BF16_ROWS
VT_ROWS = N_KV_HEADS * V_ROWS
Q_SCALE = HEAD_DIM ** -0.5 * 1.4426950408889634
SUB = 256
TM_FFN = 1024
TM_PROJ = 2048
TM_MERGE = 1024
TQ = CTX_LEN
KEY_CHUNK = 256
VMEM_LIMIT = 56 * 1024 * 1024

F32 = jnp.float32
BF16 = jnp.bfloat16


def _dot(a, b):
    return jnp.dot(a, b, preferred_element_type=F32)


def _rms(x, g):
    return x * lax.rsqrt(jnp.mean(x * x, axis=-1, keepdims=True) + EPS) * g


def _modulate(x, g, mod_ref, base):
    shift = mod_ref[base:base + 1, :]
    scale = mod_ref[base + 1:base + 2, :]
    return _rms(x, g) * (1.0 + scale) + shift


def _const_spec(shape):
    zeros = (0,) * len(shape)
    return pl.BlockSpec(shape, lambda *_: zeros, pipeline_mode=pl.Buffered(1))


def _layer_spec(shape, layer, col_block=0):
    index = (layer,) + (0,) * (len(shape) - 1) + (col_block,)
    return pl.BlockSpec((None,) + tuple(shape), lambda *_: index, pipeline_mode=pl.Buffered(1))


def _row_spec(tm, width):
    return pl.BlockSpec((tm, width), lambda r: (r, 0))


def _col_spec(height, tm):
    return pl.BlockSpec((height, tm), lambda r: (0, r))


def _mod_spec(tm):
    lat_tiles = LAT_ROWS // tm
    per_batch = SEQ // tm
    return pl.BlockSpec((None, N_MOD, D_MODEL),
                        lambda r: (jnp.where(r >= lat_tiles, BATCH, r // per_batch), 0, 0))


def _params():
    return pltpu.CompilerParams(vmem_limit_bytes=VMEM_LIMIT)


def _software_pipeline(stages, n_sub):
    carry = [None] * n_sub
    for step in range(n_sub + len(stages) - 1):
        for s in reversed(range(len(stages))):
            j = step - s
            if 0 <= j < n_sub:
                carry[j] = stages[s](j, carry[j])


def _adaln_kernel(cond_ref, w_ref, b_ref, o_ref):
    cond = cond_ref[...]
    s = (cond * jax.nn.sigmoid(cond)).astype(BF16)
    o_ref[...] = _dot(s, w_ref[...].astype(BF16)) + b_ref[...]


def _adaln(cond, w_ada, b_ada):
    return pl.pallas_call(
        _adaln_kernel,
        grid=(DEPTH, N_MOD),
        in_specs=[
            pl.BlockSpec((COND_ROWS, D_MODEL), lambda l, j: (0, 0)),
            pl.BlockSpec((None, D_MODEL, D_MODEL), lambda l, j: (l, 0, j)),
            pl.BlockSpec((None, 1, D_MODEL), lambda l, j: (l, 0, j)),
        ],
        out_specs=pl.BlockSpec((None, COND_ROWS, D_MODEL), lambda l, j: (l, 0, j)),
        out_shape=jax.ShapeDtypeStruct((DEPTH, COND_ROWS, N_MOD * D_MODEL), F32),
        compiler_params=_params(),
        name="adaln",
    )(cond, w_ada, b_ada.reshape(DEPTH, 1, N_MOD * D_MODEL))


def _ffn_kernel(*refs, mod_base, final, split_input):
    if split_input:
        lat_ref, ctx_ref, mod_ref, g_ref, w1_ref, w3_ref, w2_ref, fg_ref, o_ref = refs
        is_lat = pl.program_id(0) < LAT_ROWS // TM_FFN
    else:
        h_ref, mod_ref, g_ref, w1_ref, w3_ref, w2_ref, fg_ref, o_ref = refs

    def load(j):
        rows = pl.ds(j * SUB, SUB)
        if split_input:
            return jnp.where(is_lat, lat_ref[rows, :], ctx_ref[rows, :])
        return h_ref[rows, :]

    def norm(j, _):
        return _modulate(load(j), g_ref[...], mod_ref, mod_base).astype(BF16)

    def up(j, xn):
        return _dot(xn, w1_ref[...]), _dot(xn, w3_ref[...])

    def act(j, ab):
        a, b = ab
        return (a * jax.nn.sigmoid(a) * b).astype(BF16)

    def down(j, u):
        return _dot(u, w2_ref[...])

    def store(j, y):
        out = load(j) + (0.5 * mod_ref[mod_base + 2:mod_base + 3, :]) * y
        if final:
            out = _rms(out, fg_ref[...])
        o_ref[pl.ds(j * SUB, SUB), :] = out

    _software_pipeline([norm, up, act, down, store], TM_FFN // SUB)


def _ffn(hs, mod, g, w13, w2, fg, *, layer, mod_base, rows, final):
    split_input = len(hs) == 2
    lat_tiles = LAT_ROWS // TM_FFN
    if split_input:
        h_specs = [pl.BlockSpec((TM_FFN, D_MODEL), lambda r: (jnp.minimum(r, lat_tiles - 1), 0)),
                   pl.BlockSpec((TM_FFN, D_MODEL), lambda r: (jnp.maximum(r - lat_tiles, 0), 0))]
    else:
        h_specs = [_row_spec(TM_FFN, D_MODEL)]
    return pl.pallas_call(
        functools.partial(_ffn_kernel, mod_base=mod_base, final=final, split_input=split_input),
        grid=(rows // TM_FFN,),
        in_specs=h_specs + [
            _mod_spec(TM_FFN),
            _const_spec((1, D_MODEL)),
            _layer_spec((D_MODEL, D_FF), layer, 0),
            _layer_spec((D_MODEL, D_FF), layer, 1),
            _layer_spec((D_FF, D_MODEL), layer),
            _const_spec((1, D_MODEL)),
        ],
        out_specs=_row_spec(TM_FFN, D_MODEL),
        out_shape=jax.ShapeDtypeStruct((rows, D_MODEL), F32),
        compiler_params=_params(),
        name="ffn",
    )(*hs, mod, g, w13, w13, w2, fg)


def _head_norm_rope(x, gmat, g, cos, sin, first_half):
    ms = _dot((x * x).astype(BF16), gmat)
    y = x * lax.rsqrt(ms + EPS) * g
    up = pltpu.roll(y, LANES - HEAD_DIM // 4, 1)
    dn = pltpu.roll(y, HEAD_DIM // 4, 1)
    return y * cos + jnp.where(first_half, up, dn) * sin


def _proj_kernel(*refs, kv_only):
    if kv_only:
        h_ref, mod_ref, g_ref, w_ref, kg_ref, cos_ref, sin_ref, gmat_ref, k_ref, vt_ref = refs
    else:
        (h_ref, mod_ref, g_ref, w_ref, qg_ref, kg_ref, cos_ref, sin_ref, gmat_ref,
         qt_ref, k_ref, vt_ref, pool_ref, sgu_ref) = refs
    lane = lax.broadcasted_iota(jnp.int32, (1, LANES), 1)
    first_half = (lane % (HEAD_DIM // 2)) < (HEAD_DIM // 4)
    base = 0 if kv_only else OFF_K

    def norm(j, _):
        return _modulate(h_ref[pl.ds(j * SUB, SUB), :], g_ref[...], mod_ref, 3).astype(BF16)

    def project(j, hn):
        return _dot(hn, w_ref[...])

    def finish(j, p):
        rows = pl.ds(j * SUB, SUB)
        gmat, cos, sin = gmat_ref[...], cos_ref[rows, :], sin_ref[rows, :]
        if not kv_only:
            for c in range(ATTN_WIDTH // LANES):
                qc = _head_norm_rope(p[:, c * LANES:(c + 1) * LANES], gmat, qg_ref[...], cos, sin, first_half)
                qt_ref[c * LANES:(c + 1) * LANES, rows] = (qc * Q_SCALE).T.astype(BF16)
            pool_ref[rows, :] = p[:, OFF_POOL:OFF_POOL + POOL_WIDTH]
            sgu_ref[rows, :] = p[:, OFF_POOL + POOL_WIDTH:OFF_GATE]
        k = _head_norm_rope(p[:, base:base + KV_WIDTH], gmat, kg_ref[...], cos, sin, first_half)
        k_ref[rows, :] = k.astype(BF16)
        vt = p[:, base + KV_WIDTH:base + 2 * KV_WIDTH].T.astype(BF16)
        ones = jnp.ones((V_ROWS - HEAD_DIM, SUB), BF16)
        for g in range(N_KV_HEADS):
            vt_ref[g * V_ROWS:g * V_ROWS + HEAD_DIM, rows] = vt[g * HEAD_DIM:(g + 1) * HEAD_DIM, :]
            vt_ref[g * V_ROWS + HEAD_DIM:(g + 1) * V_ROWS, rows] = ones

    _software_pipeline([norm, project, finish], TM_PROJ // SUB)


def _proj(h, mod, g, w, qg, kg, cos, sin, gmat, *, first_tile, n_tiles, kv_only):
    tm = TM_PROJ
    lat_tiles = LAT_ROWS // tm
    per_batch = SEQ // tm
    rows = n_tiles * tm
    h_spec = pl.BlockSpec((tm, D_MODEL), lambda r: (r + first_tile, 0))
    mod_spec = pl.BlockSpec(
        (None, N_MOD, D_MODEL),
        lambda r: (jnp.where(r + first_tile >= lat_tiles, BATCH, (r + first_tile) // per_batch), 0, 0))
    tab_spec = pl.BlockSpec(
        (tm, LANES), lambda r: (jnp.where(r + first_tile >= lat_tiles, per_batch, (r + first_tile) % per_batch), 0))
    gains = [_const_spec((1, LANES))] * (1 if kv_only else 2)
    in_specs = [h_spec, mod_spec, _const_spec((1, D_MODEL)), _const_spec(w.shape)] + gains + [
        tab_spec, tab_spec, _const_spec((LANES, LANES))]
    out_specs = [_row_spec(tm, KV_WIDTH), _col_spec(VT_ROWS, tm)]
    out_shape = [jax.ShapeDtypeStruct((rows, KV_WIDTH), BF16), jax.ShapeDtypeStruct((VT_ROWS, rows), BF16)]
    if not kv_only:
        out_specs = [_col_spec(ATTN_WIDTH, tm)] + out_specs + [_row_spec(tm, POOL_WIDTH), _row_spec(tm, 2 * SGU_WIDTH)]
        out_shape = ([jax.ShapeDtypeStruct((ATTN_WIDTH, rows), BF16)] + out_shape
                     + [jax.ShapeDtypeStruct((rows, POOL_WIDTH), F32),
                        jax.ShapeDtypeStruct((rows, 2 * SGU_WIDTH), F32)])
    args = (h, mod, g, w) + ((kg,) if kv_only else (qg, kg)) + (cos, sin, gmat)
    return pl.pallas_call(
        functools.partial(_proj_kernel, kv_only=kv_only),
        grid=(n_tiles,),
        in_specs=in_specs,
        out_specs=out_specs,
        out_shape=out_shape,
        compiler_params=_params(),
        name="proj_kv" if kv_only else "proj",
    )(*args)


def _attend(qt_ref, k_refs, vt_refs, o_ref):
    zeros = jnp.zeros((HEAD_DIM, TQ), BF16)
    w = []
    for h in range(N_HEADS):
        qh = qt_ref[h * HEAD_DIM:(h + 1) * HEAD_DIM, :]
        w.append(jnp.concatenate([qh, zeros] if h < GROUP_HEADS else [zeros, qh], axis=0))

    chunks = [(k_ref, vt_ref, c * KEY_CHUNK, min(KEY_CHUNK, k_ref.shape[0] - c * KEY_CHUNK))
              for k_ref, vt_ref in zip(k_refs, vt_refs)
              for c in range(pl.cdiv(k_ref.shape[0], KEY_CHUNK))]

    def scores(chunk, h):
        k_ref, _, start, size = chunk
        return _dot(k_ref[start:start + size, :], w[h])

    m = [None] * N_HEADS
    acc = [None] * N_HEADS
    s_next = [scores(chunks[0], h) for h in range(N_HEADS)]
    for c, (k_ref, vt_ref, start, size) in enumerate(chunks):
        s = s_next
        s_next = [None] * N_HEADS
        for h in range(N_HEADS):
            if c + 1 < len(chunks):
                s_next[h] = scores(chunks[c + 1], h)
            m_chunk = jnp.max(s[h], axis=0, keepdims=True)
            m_new = m_chunk if c == 0 else jnp.maximum(m[h], m_chunk)
            p = jnp.exp2(s[h] - m_new).astype(BF16)
            base = (h // GROUP_HEADS) * V_ROWS
            pv = _dot(vt_ref[base:base + V_ROWS, start:start + size], p)
            acc[h] = pv if c == 0 else acc[h] * jnp.exp2(m[h] - m_new) + pv
            m[h] = m_new
    outs = [a[:HEAD_DIM, :] / a[HEAD_DIM:HEAD_DIM + 1, :] for a in acc]
    o_ref[...] = jnp.concatenate(outs, axis=0).T.astype(BF16)


def _attn_kernel(qt_ref, kl_ref, kc_ref, vtl_ref, vtc_ref, o_ref, *, with_ctx_queries):
    if with_ctx_queries:
        i = pl.program_id(1)

        @pl.when(i < SEQ // TQ)
        def _():
            _attend(qt_ref, [kl_ref, kc_ref], [vtl_ref, vtc_ref], o_ref)

        @pl.when(i == SEQ // TQ)
        def _():
            _attend(qt_ref, [kc_ref], [vtc_ref], o_ref)
    else:
        _attend(qt_ref, [kl_ref, kc_ref], [vtl_ref, vtc_ref], o_ref)


def _attention(qt, k_lat, vt_lat, k_ctx, vt_ctx, *, ctx_block, with_ctx_queries):
    per_batch = SEQ // TQ
    n_q = per_batch + (1 if with_ctx_queries else 0)
    rows = ALL_ROWS if with_ctx_queries else LAT_ROWS

    def q_block(b, i):
        return jnp.where(i < per_batch, b * per_batch + i, LAT_ROWS // TQ + b)

    return pl.pallas_call(
        functools.partial(_attn_kernel, with_ctx_queries=with_ctx_queries),
        grid=(BATCH, n_q),
        in_specs=[
            pl.BlockSpec((ATTN_WIDTH, TQ), lambda b, i: (0, q_block(b, i))),
            pl.BlockSpec((SEQ, KV_WIDTH), lambda b, i: (b, 0)),
            pl.BlockSpec((CTX_LEN, KV_WIDTH), lambda b, i: (ctx_block + b, 0)),
            pl.BlockSpec((VT_ROWS, SEQ), lambda b, i: (0, b)),
            pl.BlockSpec((VT_ROWS, CTX_LEN), lambda b, i: (0, ctx_block + b)),
        ],
        out_specs=pl.BlockSpec((TQ, ATTN_WIDTH), lambda b, i: (q_block(b, i), 0)),
        out_shape=jax.ShapeDtypeStruct((rows, ATTN_WIDTH), BF16),
        compiler_params=_params(),
        name="attention",
    )(qt, k_lat, k_ctx, vt_lat, vt_ctx)


def _pool_branch(j, pool_ref, prev_ref, next_ref, inv_ref, xe_ref, q2_ref, q4_ref, q8_ref):
    tm = SUB
    n_sub = TM_MERGE // SUB
    sub = pl.program_id(0) * n_sub + j
    is_ctx = sub >= LAT_ROWS // SUB
    per_seg = jnp.where(is_ctx, CTX_LEN // SUB, SEQ // SUB)
    seg_idx = jnp.where(is_ctx, sub - LAT_ROWS // SUB, sub) % per_seg
    first = seg_idx == 0
    last = seg_idx == per_seg - 1
    x = pool_ref[pl.ds(j * SUB, SUB), :]
    prev_rows = prev_ref[...] if j == 0 else pool_ref[pl.ds(j * SUB - POOL_HALO, POOL_HALO), :]
    next_rows = next_ref[...] if j == n_sub - 1 else pool_ref[pl.ds((j + 1) * SUB, POOL_HALO), :]
    xe_ref[0:POOL_HALO, :] = jnp.where(first, 0.0, prev_rows)
    xe_ref[POOL_HALO:POOL_HALO + tm, :] = x
    xe_ref[POOL_HALO + tm:2 * POOL_HALO + tm, :] = jnp.where(last, 0.0, next_rows)
    xe_ref[2 * POOL_HALO + tm:, :] = jnp.zeros((2 * POOL_HALO, POOL_WIDTH), F32)
    q2_ref[...] = xe_ref[0:tm + 24, :] + xe_ref[1:tm + 25, :]
    q4_ref[...] = q2_ref[0:tm + 16, :] + q2_ref[2:tm + 18, :]
    q8_ref[...] = q4_ref[0:tm + 8, LANES:] + q4_ref[4:tm + 12, LANES:]
    q16 = q8_ref[0:tm, :] + q8_ref[8:tm + 8, :]
    lane = lax.broadcasted_iota(jnp.int32, (1, LANES), 1)
    low = lane < POOL_GROUP_DIM
    sums = jnp.concatenate([
        jnp.where(low, q2_ref[7:7 + tm, 0:LANES], q4_ref[6:6 + tm, 0:LANES]),
        jnp.where(low, q8_ref[4:4 + tm, :], q16),
    ], axis=1)
    return (sums * inv_ref[pl.ds(j * SUB, SUB), :] - x).astype(BF16)


def _gelu_tanh(x):
    return x * (0.5 * (1.0 + jnp.tanh(0.7978845608028654 * (x + 0.044715 * (x * x * x)))))


def _sgu_prepare(j, sgu_ref, norm_ref):
    gl = _gelu_tanh(sgu_ref[pl.ds(j * SUB, SUB), :])
    u = gl[:, :SGU_WIDTH]
    v = _rms(gl[:, SGU_WIDTH:], norm_ref[...])
    lane = lax.broadcasted_iota(jnp.int32, (1, SGU_WIDTH), 1)
    stacks = []
    for c in range(SUB // CHUNK):
        vc = v[c * CHUNK:(c + 1) * CHUNK, :]
        stacks.append(jnp.concatenate(
            [jnp.where((lane >= g * SGU_GROUP_DIM) & (lane < (g + 1) * SGU_GROUP_DIM), vc, 0.0)
             for g in range(SGU_GROUPS)], axis=0).astype(BF16))
    return u, stacks


def _merge_kernel(h_ref, mod_ref, g_ref, attn_ref, pool_ref, prev_ref, next_ref, inv_ref, sgu_ref,
                  wgate_ref, wbd_ref, pscale_ref, snorm_ref, wcat_ref, sbias_ref,
                  wbp_ref, wba_ref, wbs_ref, wout_ref, o_ref,
                  xe_ref, q2_ref, q4_ref, q8_ref):
    def prepare(j, _):
        hn = _modulate(h_ref[pl.ds(j * SUB, SUB), :], g_ref[...], mod_ref, 3).astype(BF16)
        pooled = _pool_branch(j, pool_ref, prev_ref, next_ref, inv_ref, xe_ref, q2_ref, q4_ref, q8_ref)
        u, stacks = _sgu_prepare(j, sgu_ref, snorm_ref)
        return hn, pooled, u, stacks

    def gates(j, carry):
        hn, pooled, u, stacks = carry
        return [_dot(hn, wgate_ref[:, i * D_MODEL:(i + 1) * D_MODEL]) for i in range(3)], pooled, u, stacks

    def branches(j, carry):
        gate_logits, pooled, u, stacks = carry
        pool_out = (_dot(pooled, wbd_ref[...]) * pscale_ref[...]).astype(BF16)
        spatial = jnp.concatenate([_dot(wcat_ref[...], st) + sbias_ref[...] for st in stacks], axis=0)
        sgu_out = (u * spatial).astype(BF16)
        merged = jax.nn.sigmoid(gate_logits[0]) * _dot(pool_out, wbp_ref[...])
        merged += jax.nn.sigmoid(gate_logits[1]) * _dot(attn_ref[pl.ds(j * SUB, SUB), :], wba_ref[...])
        merged += jax.nn.sigmoid(gate_logits[2]) * _dot(sgu_out, wbs_ref[...])
        return merged.astype(BF16)

    def store(j, merged):
        rows = pl.ds(j * SUB, SUB)
        o_ref[rows, :] = h_ref[rows, :] + mod_ref[5:6, :] * _dot(merged, wout_ref[...])

    _software_pipeline([prepare, gates, branches, store], TM_MERGE // SUB)


def _merge(h, mod, g, attn, pool_in, pool_inv, sgu_in, wgate, wbd, pscale, snorm, wcat, sbias, wbp, wba, wbs, wout,
           *, layer, rows):
    tm = TM_MERGE
    blocks_per_tile = tm // POOL_HALO
    last_block = rows // POOL_HALO - 1
    prev_spec = pl.BlockSpec((POOL_HALO, POOL_WIDTH), lambda r: (jnp.maximum(r * blocks_per_tile - 1, 0), 0))
    next_spec = pl.BlockSpec((POOL_HALO, POOL_WIDTH),
                             lambda r: (jnp.minimum((r + 1) * blocks_per_tile, last_block), 0))
    lat_tiles = LAT_ROWS // tm
    per_batch = SEQ // tm
    inv_spec = pl.BlockSpec((tm, POOL_WIDTH), lambda r: (jnp.where(r >= lat_tiles, per_batch, r % per_batch), 0))
    return pl.pallas_call(
        _merge_kernel,
        grid=(rows // tm,),
        in_specs=[
            _row_spec(tm, D_MODEL),
            _mod_spec(tm),
            _const_spec((1, D_MODEL)),
            _row_spec(tm, ATTN_WIDTH),
            _row_spec(tm, POOL_WIDTH),
            prev_spec,
            next_spec,
            inv_spec,
            _row_spec(tm, 2 * SGU_WIDTH),
            _const_spec((D_MODEL, 3 * D_MODEL)),
            _const_spec((POOL_WIDTH, POOL_WIDTH)),
            _const_spec((1, POOL_WIDTH)),
            _const_spec((1, SGU_WIDTH)),
            _const_spec((CHUNK, SGU_GROUPS * CHUNK)),
            _const_spec((CHUNK, SGU_WIDTH)),
            _layer_spec((POOL_WIDTH, D_MODEL), layer),
            _layer_spec((ATTN_WIDTH, D_MODEL), layer),
            _layer_spec((SGU_WIDTH, D_MODEL), layer),
            _layer_spec((D_MODEL, D_MODEL), layer),
        ],
        out_specs=_row_spec(tm, D_MODEL),
        out_shape=jax.ShapeDtypeStruct((rows, D_MODEL), F32),
        scratch_shapes=[
            pltpu.VMEM((SUB + 32, POOL_WIDTH), F32),
            pltpu.VMEM((SUB + 24, POOL_WIDTH), F32),
            pltpu.VMEM((SUB + 16, POOL_WIDTH), F32),
            pltpu.VMEM((SUB + 8, LANES), F32),
        ],
        compiler_params=_params(),
        name="merge",
    )(h, mod, g, attn, pool_in, pool_in, pool_in, pool_inv, sgu_in, wgate, wbd, pscale, snorm, wcat, sbias,
      wbp, wba, wbs, wout)


def _rope_tables():
    pos = jnp.arange(SEQ, dtype=jnp.int32)
    row = (pos // GRID_W).astype(F32)[:, None]
    col = (pos % GRID_W).astype(F32)[:, None]
    half = HEAD_DIM // 2
    inv_freq = ROPE_THETA ** (-jnp.arange(0, half, 2, dtype=F32) / half)[None, :]
    ang = jnp.concatenate([row * inv_freq, row * inv_freq, col * inv_freq, col * inv_freq], axis=1)
    sign = jnp.tile(jnp.concatenate([-jnp.ones((half // 2,), F32), jnp.ones((half // 2,), F32)]), 2)[None, :]
    cos = jnp.concatenate([jnp.cos(ang), jnp.ones((TM_PROJ, HEAD_DIM), F32)], axis=0)
    sin = jnp.concatenate([jnp.sin(ang) * sign, jnp.zeros((TM_PROJ, HEAD_DIM), F32)], axis=0)
    return jnp.tile(cos, (1, LANES // HEAD_DIM)), jnp.tile(sin, (1, LANES // HEAD_DIM))


def _pool_inverse_counts():
    half = jnp.repeat(jnp.array([w // 2 for w in POOL_WINDOWS], jnp.int32), POOL_GROUP_DIM)[None, :]

    def table(seg_len):
        pos = jnp.arange(seg_len, dtype=jnp.int32)[:, None]
        cnt = jnp.minimum(pos + half, seg_len) - jnp.maximum(pos - half, 0)
        return 1.0 / cnt.astype(F32)

    return jnp.concatenate([table(SEQ), jnp.tile(table(CTX_LEN), (TM_MERGE // CTX_LEN, 1))], axis=0)


def _block_diag(blocks):
    n, r, c = blocks.shape
    eye = jnp.eye(n, dtype=blocks.dtype)
    return (eye[:, None, :, None] * blocks[:, :, None, :]).reshape(n * r, n * c)


def kernel(x, c, ctx, c_ctx, w_ada, b_ada, norm_ffn1, ffn1_w13, ffn1_w2, norm_mix, w_in, q_norm, k_norm,
           pool_w, pool_scale, sgu_norm, sgu_w, sgu_b, w_br_pool, w_br_attn, w_br_sgu, w_out, norm_ffn2,
           ffn2_w13, ffn2_w2, final_norm):
    cond = jnp.concatenate([c, c_ctx[None, :], jnp.zeros((COND_ROWS - BATCH - 1, D_MODEL), F32)], axis=0)
    mods = _adaln(cond, w_ada, b_ada).reshape(DEPTH, COND_ROWS, N_MOD, D_MODEL)
    cos, sin = _rope_tables()
    gmat = _block_diag(jnp.full((LANES // HEAD_DIM, HEAD_DIM, HEAD_DIM), 1.0 / HEAD_DIM, F32)).astype(BF16)
    fg = final_norm.reshape(1, D_MODEL)
    pool_inv = _pool_inverse_counts()
    lat_tiles = LAT_ROWS // TM_PROJ
    ctx_tiles = CTX_ROWS // TM_PROJ

    ffn1_w13_bf, ffn1_w2_bf = ffn1_w13.astype(BF16), ffn1_w2.astype(BF16)
    ffn2_w13_bf, ffn2_w2_bf = ffn2_w13.astype(BF16), ffn2_w2.astype(BF16)
    w_br_pool_bf, w_br_attn_bf = w_br_pool.astype(BF16), w_br_attn.astype(BF16)
    w_br_sgu_bf, w_out_bf = w_br_sgu.astype(BF16), w_out.astype(BF16)

    hs = (x.reshape(LAT_ROWS, D_MODEL), ctx.reshape(CTX_ROWS, D_MODEL))
    for l in range(DEPTH):
        last = l == DEPTH - 1
        mod = mods[l]
        g_mix = norm_mix[l].reshape(1, D_MODEL)
        qg = jnp.tile(q_norm[l], LANES // HEAD_DIM).reshape(1, LANES)
        kg = jnp.tile(k_norm[l], LANES // HEAD_DIM).reshape(1, LANES)
        h = _ffn(hs, mod, norm_ffn1[l].reshape(1, D_MODEL), ffn1_w13_bf, ffn1_w2_bf,
                 fg, layer=l, mod_base=0, rows=ALL_ROWS, final=False)
        w_proj = w_in[l, :, :OFF_GATE].astype(BF16)
        if last:
            qt, k, vt, pool_in, sgu_in = _proj(h, mod, g_mix, w_proj, qg, kg, cos, sin, gmat,
                                               first_tile=0, n_tiles=lat_tiles, kv_only=False)
            k_ctx, vt_ctx = _proj(h, mod, g_mix, w_in[l, :, OFF_K:OFF_POOL].astype(BF16), None, kg, cos, sin, gmat,
                                  first_tile=lat_tiles, n_tiles=ctx_tiles, kv_only=True)
            attn = _attention(qt, k, vt, k_ctx, vt_ctx, ctx_block=0, with_ctx_queries=False)
            rows = LAT_ROWS
        else:
            qt, k, vt, pool_in, sgu_in = _proj(h, mod, g_mix, w_proj, qg, kg, cos, sin, gmat,
                                               first_tile=0, n_tiles=lat_tiles + ctx_tiles, kv_only=False)
            attn = _attention(qt, k, vt, k, vt, ctx_block=LAT_ROWS // CTX_LEN, with_ctx_queries=True)
            rows = ALL_ROWS
        h = _merge(
            h, mod, g_mix, attn, pool_in, pool_inv, sgu_in, w_in[l, :, OFF_GATE:].astype(BF16),
            _block_diag(pool_w[l]).astype(BF16), pool_scale[l].reshape(1, POOL_WIDTH),
            sgu_norm[l].reshape(1, SGU_WIDTH),
            jnp.transpose(sgu_w[l], (1, 0, 2)).reshape(CHUNK, SGU_GROUPS * CHUNK).astype(BF16),
            jnp.repeat(sgu_b[l].T, SGU_GROUP_DIM, axis=1),
            w_br_pool_bf, w_br_attn_bf, w_br_sgu_bf, w_out_bf, layer=l, rows=rows)
        hs = (_ffn((h,), mod, norm_ffn2[l].reshape(1, D_MODEL), ffn2_w13_bf,
                   ffn2_w2_bf, fg, layer=l, mod_base=6, rows=rows, final=last),)
    return hs[0].reshape(BATCH, SEQ, D_MODEL)
```

```python
import functools

import jax
import jax.numpy as jnp
from jax import lax
from jax.experimental import pallas as pl
from jax.experimental.pallas import tpu as pltpu

D_MODEL = 1024
BATCH = 16
SEQ = 2048
DEPTH = 2
GRID_W = 64
CTX_LEN = 256
N_HEADS = 8
N_KV_HEADS = 2
HEAD_DIM = 64
ATTN_WIDTH = N_HEADS * HEAD_DIM
KV_WIDTH = N_KV_HEADS * HEAD_DIM
ROPE_THETA = 10000.0
POOL_WINDOWS = (2, 4, 8, 16)
POOL_GROUPS = 4
POOL_GROUP_DIM = 64
POOL_WIDTH = POOL_GROUPS * POOL_GROUP_DIM
SGU_GROUPS = 4
SGU_GROUP_DIM = 64
SGU_WIDTH = SGU_GROUPS * SGU_GROUP_DIM
CHUNK = 128
D_FF = 2816
EPS = 1e-6
N_MOD = 9
OFF_K = ATTN_WIDTH
OFF_POOL = ATTN_WIDTH + 2 * KV_WIDTH
OFF_GATE = OFF_POOL + POOL_WIDTH + 2 * SGU_WIDTH

LAT_ROWS = BATCH * SEQ
CTX_ROWS = BATCH * CTX_LEN
ALL_ROWS = LAT_ROWS + CTX_ROWS
GROUP_HEADS = N_HEADS // N_KV_HEADS
LANES = 128
COND_ROWS = 24
POOL_HALO = max(POOL_WINDOWS) // 2
BF16_ROWS = 16
V_ROWS = HEAD_DIM + BF16_ROWS
VT_ROWS = N_KV_HEADS * V_ROWS
Q_SCALE = HEAD_DIM ** -0.5 * 1.4426950408889634
SUB = 256
TM_FFN = 1024
TM_PROJ = 2048
TM_MERGE = 1024
TQ = CTX_LEN
KEY_CHUNK = 256
VMEM_LIMIT = 56 * 1024 * 1024

F32 = jnp.float32
BF16 = jnp.bfloat16


def _dot(a, b):
    return jnp.dot(a, b, preferred_element_type=F32)


def _rms(x, g):
    return x * lax.rsqrt(jnp.mean(x * x, axis=-1, keepdims=True) + EPS) * g


def _modulate(x, g, mod_ref, base):
    shift = mod_ref[base:base + 1, :]
    scale = mod_ref[base + 1:base + 2, :]
    return _rms(x, g * (1.0 + scale)) + shift


def _const_spec(shape):
    zeros = (0,) * len(shape)
    return pl.BlockSpec(shape, lambda *_: zeros, pipeline_mode=pl.Buffered(1))


def _layer_spec(shape, layer):
    index = (layer,) + (0,) * len(shape)
    return pl.BlockSpec((None,) + tuple(shape), lambda *_: index, pipeline_mode=pl.Buffered(1))


def _row_spec(tm, width):
    return pl.BlockSpec((tm, width), lambda r: (r, 0))


def _col_spec(height, tm):
    return pl.BlockSpec((height, tm), lambda r: (0, r))


def _mod_spec(tm):
    lat_tiles = LAT_ROWS // tm
    per_batch = SEQ // tm
    return pl.BlockSpec((None, N_MOD, D_MODEL),
                        lambda r: (jnp.where(r >= lat_tiles, BATCH, r // per_batch), 0, 0))


def _params():
    return pltpu.CompilerParams(vmem_limit_bytes=VMEM_LIMIT)


def _software_pipeline(stages, n_sub):
    carry = [None] * n_sub
    for step in range(n_sub + len(stages) - 1):
        for s in reversed(range(len(stages))):
            j = step - s
            if 0 <= j < n_sub:
                carry[j] = stages[s](j, carry[j])


def _adaln_kernel(cond_ref, w_ref, b_ref, o_ref):
    cond = cond_ref[...]
    s = (cond * jax.nn.sigmoid(cond)).astype(BF16)
    o_ref[...] = _dot(s, w_ref[...].astype(BF16)) + b_ref[...]


def _adaln(cond, w_ada, b_ada):
    return pl.pallas_call(
        _adaln_kernel,
        grid=(DEPTH, N_MOD),
        in_specs=[
            pl.BlockSpec((COND_ROWS, D_MODEL), lambda l, j: (0, 0)),
            pl.BlockSpec((None, D_MODEL, D_MODEL), lambda l, j: (l, 0, j)),
            pl.BlockSpec((None, 1, D_MODEL), lambda l, j: (l, 0, j)),
        ],
        out_specs=pl.BlockSpec((None, COND_ROWS, D_MODEL), lambda l, j: (l, 0, j)),
        out_shape=jax.ShapeDtypeStruct((DEPTH, COND_ROWS, N_MOD * D_MODEL), F32),
        compiler_params=_params(),
        name="adaln",
    )(cond, w_ada, b_ada.reshape(DEPTH, 1, N_MOD * D_MODEL))


def _ffn_kernel(*refs, mod_base, final, split_input):
    if split_input:
        lat_ref, ctx_ref, mod_ref, g_ref, w13_ref, w2_ref, fg_ref, o_ref = refs
        is_lat = pl.program_id(0) < LAT_ROWS // TM_FFN
    else:
        h_ref, mod_ref, g_ref, w13_ref, w2_ref, fg_ref, o_ref = refs

    def load(j):
        rows = pl.ds(j * SUB, SUB)
        if split_input:
            return jnp.where(is_lat, lat_ref[rows, :], ctx_ref[rows, :])
        return h_ref[rows, :]

    def norm(j, _):
        return _modulate(load(j), g_ref[...], mod_ref, mod_base).astype(BF16)

    def up(j, xn):
        return _dot(xn, w13_ref[:, :D_FF]), _dot(xn, w13_ref[:, D_FF:])

    def act(j, ab):
        a, b = ab
        return (a * jax.nn.sigmoid(a) * b).astype(BF16)

    def down(j, u):
        return _dot(u, w2_ref[...])

    def store(j, y):
        out = load(j) + (0.5 * mod_ref[mod_base + 2:mod_base + 3, :]) * y
        if final:
            out = _rms(out, fg_ref[...])
        o_ref[pl.ds(j * SUB, SUB), :] = out

    _software_pipeline([norm, up, act, down, store], TM_FFN // SUB)


def _ffn(hs, mod, g, w13, w2, fg, *, layer, mod_base, rows, final):
    split_input = len(hs) == 2
    lat_tiles = LAT_ROWS // TM_FFN
    if split_input:
        h_specs = [pl.BlockSpec((TM_FFN, D_MODEL), lambda r: (jnp.minimum(r, lat_tiles - 1), 0)),
                   pl.BlockSpec((TM_FFN, D_MODEL), lambda r: (jnp.maximum(r - lat_tiles, 0), 0))]
    else:
        h_specs = [_row_spec(TM_FFN, D_MODEL)]
    return pl.pallas_call(
        functools.partial(_ffn_kernel, mod_base=mod_base, final=final, split_input=split_input),
        grid=(rows // TM_FFN,),
        in_specs=h_specs + [
            _mod_spec(TM_FFN),
            _const_spec((1, D_MODEL)),
            _layer_spec((D_MODEL, 2 * D_FF), layer),
            _layer_spec((D_FF, D_MODEL), layer),
            _const_spec((1, D_MODEL)),
        ],
        out_specs=_row_spec(TM_FFN, D_MODEL),
        out_shape=jax.ShapeDtypeStruct((rows, D_MODEL), F32),
        compiler_params=_params(),
        name="ffn",
    )(*hs, mod, g, w13, w2, fg)


def _head_norm_rope(x, gmat, g, cos, sin, first_half):
    ms = _dot((x * x).astype(BF16), gmat)
    y = x * lax.rsqrt(ms + EPS) * g
    up = pltpu.roll(y, LANES - HEAD_DIM // 4, 1)
    dn = pltpu.roll(y, HEAD_DIM // 4, 1)
    return y * cos + jnp.where(first_half, up, dn) * sin


def _proj_kernel(*refs, kv_only):
    if kv_only:
        h_ref, mod_ref, g_ref, w_ref, kg_ref, cos_ref, sin_ref, gmat_ref, k_ref, vt_ref = refs
    else:
        (h_ref, mod_ref, g_ref, w_ref, qg_ref, kg_ref, cos_ref, sin_ref, gmat_ref,
         qt_ref, k_ref, vt_ref, pool_ref, sgu_ref) = refs
    lane = lax.broadcasted_iota(jnp.int32, (1, LANES), 1)
    first_half = (lane % (HEAD_DIM // 2)) < (HEAD_DIM // 4)
    base = 0 if kv_only else OFF_K

    def norm(j, _):
        return _modulate(h_ref[pl.ds(j * SUB, SUB), :], g_ref[...], mod_ref, 3).astype(BF16)

    def project(j, hn):
        return _dot(hn, w_ref[...])

    def finish(j, p):
        rows = pl.ds(j * SUB, SUB)
        gmat, cos, sin = gmat_ref[...], cos_ref[rows, :], sin_ref[rows, :]
        if not kv_only:
            for c in range(ATTN_WIDTH // LANES):
                qc = _head_norm_rope(p[:, c * LANES:(c + 1) * LANES], gmat, qg_ref[...], cos, sin, first_half)
                qt_ref[c * LANES:(c + 1) * LANES, rows] = (qc * Q_SCALE).T.astype(BF16)
            pool_ref[rows, :] = p[:, OFF_POOL:OFF_POOL + POOL_WIDTH]
            sgu_ref[rows, :] = p[:, OFF_POOL + POOL_WIDTH:OFF_GATE]
        k = _head_norm_rope(p[:, base:base + KV_WIDTH], gmat, kg_ref[...], cos, sin, first_half)
        k_ref[rows, :] = k.astype(BF16)
        vt = p[:, base + KV_WIDTH:base + 2 * KV_WIDTH].T.astype(BF16)
        ones = jnp.ones((V_ROWS - HEAD_DIM, SUB), BF16)
        for g in range(N_KV_HEADS):
            vt_ref[g * V_ROWS:g * V_ROWS + HEAD_DIM, rows] = vt[g * HEAD_DIM:(g + 1) * HEAD_DIM, :]
            vt_ref[g * V_ROWS + HEAD_DIM:(g + 1) * V_ROWS, rows] = ones

    _software_pipeline([norm, project, finish], TM_PROJ // SUB)


def _proj(h, mod, g, w, qg, kg, cos, sin, gmat, *, first_tile, n_tiles, kv_only):
    tm = TM_PROJ
    lat_tiles = LAT_ROWS // tm
    per_batch = SEQ // tm
    rows = n_tiles * tm
    h_spec = pl.BlockSpec((tm, D_MODEL), lambda r: (r + first_tile, 0))
    mod_spec = pl.BlockSpec(
        (None, N_MOD, D_MODEL),
        lambda r: (jnp.where(r + first_tile >= lat_tiles, BATCH, (r + first_tile) // per_batch), 0, 0))
    tab_spec = pl.BlockSpec(
        (tm, LANES), lambda r: (jnp.where(r + first_tile >= lat_tiles, per_batch, (r + first_tile) % per_batch), 0))
    gains = [_const_spec((1, LANES))] * (1 if kv_only else 2)
    in_specs = [h_spec, mod_spec, _const_spec((1, D_MODEL)), _const_spec(w.shape)] + gains + [
        tab_spec, tab_spec, _const_spec((LANES, LANES))]
    out_specs = [_row_spec(tm, KV_WIDTH), _col_spec(VT_ROWS, tm)]
    out_shape = [jax.ShapeDtypeStruct((rows, KV_WIDTH), BF16), jax.ShapeDtypeStruct((VT_ROWS, rows), BF16)]
    if not kv_only:
        out_specs = [_col_spec(ATTN_WIDTH, tm)] + out_specs + [_row_spec(tm, POOL_WIDTH), _row_spec(tm, 2 * SGU_WIDTH)]
        out_shape = ([jax.ShapeDtypeStruct((ATTN_WIDTH, rows), BF16)] + out_shape
                     + [jax.ShapeDtypeStruct((rows, POOL_WIDTH), F32),
                        jax.ShapeDtypeStruct((rows, 2 * SGU_WIDTH), F32)])
    args = (h, mod, g, w) + ((kg,) if kv_only else (qg, kg)) + (cos, sin, gmat)
    return pl.pallas_call(
        functools.partial(_proj_kernel, kv_only=kv_only),
        grid=(n_tiles,),
        in_specs=in_specs,
        out_specs=out_specs,
        out_shape=out_shape,
        compiler_params=_params(),
        name="proj_kv" if kv_only else "proj",
    )(*args)


def _attend(qt_ref, k_refs, vt_refs, o_ref):
    zeros = jnp.zeros((HEAD_DIM, TQ), BF16)
    w = []
    for h in range(N_HEADS):
        qh = qt_ref[h * HEAD_DIM:(h + 1) * HEAD_DIM, :]
        w.append(jnp.concatenate([qh, zeros] if h < GROUP_HEADS else [zeros, qh], axis=0))

    chunks = [(k_ref, vt_ref, c * KEY_CHUNK, min(KEY_CHUNK, k_ref.shape[0] - c * KEY_CHUNK))
              for k_ref, vt_ref in zip(k_refs, vt_refs)
              for c in range(pl.cdiv(k_ref.shape[0], KEY_CHUNK))]

    def scores(chunk, h):
        k_ref, _, start, size = chunk
        return _dot(k_ref[start:start + size, :], w[h])

    m = [None] * N_HEADS
    acc = [None] * N_HEADS
    s_next = [scores(chunks[0], h) for h in range(N_HEADS)]
    for c, (k_ref, vt_ref, start, size) in enumerate(chunks):
        s = s_next
        s_next = [None] * N_HEADS
        for h in range(N_HEADS):
            if c + 1 < len(chunks):
                s_next[h] = scores(chunks[c + 1], h)
            m_chunk = jnp.max(s[h], axis=0, keepdims=True)
            m_new = m_chunk if c == 0 else jnp.maximum(m[h], m_chunk)
            p = jnp.exp2(s[h] - m_new).astype(BF16)
            base = (h // GROUP_HEADS) * V_ROWS
            pv = _dot(vt_ref[base:base + V_ROWS, start:start + size], p)
            acc[h] = pv if c == 0 else acc[h] * jnp.exp2(m[h] - m_new) + pv
            m[h] = m_new
    outs = [a[:HEAD_DIM, :] / a[HEAD_DIM:HEAD_DIM + 1, :] for a in acc]
    o_ref[...] = jnp.concatenate(outs, axis=0).T.astype(BF16)


def _attn_kernel(qt_ref, kl_ref, kc_ref, vtl_ref, vtc_ref, o_ref, *, with_ctx_queries):
    if with_ctx_queries:
        i = pl.program_id(1)

        @pl.when(i < SEQ // TQ)
        def _():
            _attend(qt_ref, [kl_ref, kc_ref], [vtl_ref, vtc_ref], o_ref)

        @pl.when(i == SEQ // TQ)
        def _():
            _attend(qt_ref, [kc_ref], [vtc_ref], o_ref)
    else:
        _attend(qt_ref, [kl_ref, kc_ref], [vtl_ref, vtc_ref], o_ref)


def _attention(qt, k_lat, vt_lat, k_ctx, vt_ctx, *, ctx_block, with_ctx_queries):
    per_batch = SEQ // TQ
    n_q = per_batch + (1 if with_ctx_queries else 0)
    rows = ALL_ROWS if with_ctx_queries else LAT_ROWS

    def q_block(b, i):
        return jnp.where(i < per_batch, b * per_batch + i, LAT_ROWS // TQ + b)

    return pl.pallas_call(
        functools.partial(_attn_kernel, with_ctx_queries=with_ctx_queries),
        grid=(BATCH, n_q),
        in_specs=[
            pl.BlockSpec((ATTN_WIDTH, TQ), lambda b, i: (0, q_block(b, i))),
            pl.BlockSpec((SEQ, KV_WIDTH), lambda b, i: (b, 0)),
            pl.BlockSpec((CTX_LEN, KV_WIDTH), lambda b, i: (ctx_block + b, 0)),
            pl.BlockSpec((VT_ROWS, SEQ), lambda b, i: (0, b)),
            pl.BlockSpec((VT_ROWS, CTX_LEN), lambda b, i: (0, ctx_block + b)),
        ],
        out_specs=pl.BlockSpec((TQ, ATTN_WIDTH), lambda b, i: (q_block(b, i), 0)),
        out_shape=jax.ShapeDtypeStruct((rows, ATTN_WIDTH), BF16),
        compiler_params=_params(),
        name="attention",
    )(qt, k_lat, k_ctx, vt_lat, vt_ctx)


def _pool_branch(j, pool_ref, prev_ref, next_ref, inv_ref, xe_ref, q2_ref, q4_ref, q8_ref):
    tm = SUB
    n_sub = TM_MERGE // SUB
    sub = pl.program_id(0) * n_sub + j
    is_ctx = sub >= LAT_ROWS // SUB
    per_seg = jnp.where(is_ctx, CTX_LEN // SUB, SEQ // SUB)
    seg_idx = jnp.where(is_ctx, sub - LAT_ROWS // SUB, sub) % per_seg
    first = seg_idx == 0
    last = seg_idx == per_seg - 1
    x = pool_ref[pl.ds(j * SUB, SUB), :]
    prev_rows = prev_ref[...] if j == 0 else pool_ref[pl.ds(j * SUB - POOL_HALO, POOL_HALO), :]
    next_rows = next_ref[...] if j == n_sub - 1 else pool_ref[pl.ds((j + 1) * SUB, POOL_HALO), :]
    xe_ref[0:POOL_HALO, :] = jnp.where(first, 0.0, prev_rows)
    xe_ref[POOL_HALO:POOL_HALO + tm, :] = x
    xe_ref[POOL_HALO + tm:2 * POOL_HALO + tm, :] = jnp.where(last, 0.0, next_rows)
    xe_ref[2 * POOL_HALO + tm:, :] = jnp.zeros((2 * POOL_HALO, POOL_WIDTH), F32)
    q2_ref[...] = xe_ref[0:tm + 24, :] + xe_ref[1:tm + 25, :]
    q4_ref[...] = q2_ref[0:tm + 16, :] + q2_ref[2:tm + 18, :]
    q8_ref[...] = q4_ref[0:tm + 8, LANES:] + q4_ref[4:tm + 12, LANES:]
    q16 = q8_ref[0:tm, :] + q8_ref[8:tm + 8, :]
    lane = lax.broadcasted_iota(jnp.int32, (1, LANES), 1)
    low = lane < POOL_GROUP_DIM
    sums = jnp.concatenate([
        jnp.where(low, q2_ref[7:7 + tm, 0:LANES], q4_ref[6:6 + tm, 0:LANES]),
        jnp.where(low, q8_ref[4:4 + tm, :], q16),
    ], axis=1)
    return (sums * inv_ref[pl.ds(j * SUB, SUB), :] - x).astype(BF16)


def _gelu_tanh(x):
    return x * (0.5 * (1.0 + jnp.tanh(0.7978845608028654 * (x + 0.044715 * (x * x * x)))))


def _sgu_prepare(j, sgu_ref, norm_ref):
    gl = _gelu_tanh(sgu_ref[pl.ds(j * SUB, SUB), :])
    u = gl[:, :SGU_WIDTH]
    v = _rms(gl[:, SGU_WIDTH:], norm_ref[...])
    lane = lax.broadcasted_iota(jnp.int32, (1, SGU_WIDTH), 1)
    stacks = []
    for c in range(SUB // CHUNK):
        vc = v[c * CHUNK:(c + 1) * CHUNK, :]
        stacks.append(jnp.concatenate(
            [jnp.where((lane >= g * SGU_GROUP_DIM) & (lane < (g + 1) * SGU_GROUP_DIM), vc, 0.0)
             for g in range(SGU_GROUPS)], axis=0).astype(BF16))
    return u, stacks


def _merge_kernel(h_ref, mod_ref, g_ref, attn_ref, pool_ref, prev_ref, next_ref, inv_ref, sgu_ref,
                  wgate_ref, wbd_ref, pscale_ref, snorm_ref, wcat_ref, sbias_ref,
                  wbp_ref, wba_ref, wbs_ref, wout_ref, o_ref,
                  xe_ref, q2_ref, q4_ref, q8_ref):
    def prepare(j, _):
        hn = _modulate(h_ref[pl.ds(j * SUB, SUB), :], g_ref[...], mod_ref, 3).astype(BF16)
        pooled = _pool_branch(j, pool_ref, prev_ref, next_ref, inv_ref, xe_ref, q2_ref, q4_ref, q8_ref)
        u, stacks = _sgu_prepare(j, sgu_ref, snorm_ref)
        return hn, pooled, u, stacks

    def gates(j, carry):
        hn, pooled, u, stacks = carry
        return [_dot(hn, wgate_ref[:, i * D_MODEL:(i + 1) * D_MODEL]) for i in range(3)], pooled, u, stacks

    def branches(j, carry):
        gate_logits, pooled, u, stacks = carry
        pool_out = (_dot(pooled, wbd_ref[...]) * pscale_ref[...]).astype(BF16)
        spatial = jnp.concatenate([_dot(wcat_ref[...], st) + sbias_ref[...] for st in stacks], axis=0)
        sgu_out = (u * spatial).astype(BF16)
        merged = jax.nn.sigmoid(gate_logits[0]) * _dot(pool_out, wbp_ref[...])
        merged += jax.nn.sigmoid(gate_logits[1]) * _dot(attn_ref[pl.ds(j * SUB, SUB), :], wba_ref[...])
        merged += jax.nn.sigmoid(gate_logits[2]) * _dot(sgu_out, wbs_ref[...])
        return merged.astype(BF16)

    def store(j, merged):
        rows = pl.ds(j * SUB, SUB)
        o_ref[rows, :] = h_ref[rows, :] + mod_ref[5:6, :] * _dot(merged, wout_ref[...])

    _software_pipeline([prepare, gates, branches, store], TM_MERGE // SUB)


def _merge(h, mod, g, attn, pool_in, pool_inv, sgu_in, wgate, wbd, pscale, snorm, wcat, sbias, wbp, wba, wbs, wout,
           *, layer, rows):
    tm = TM_MERGE
    blocks_per_tile = tm // POOL_HALO
    last_block = rows // POOL_HALO - 1
    prev_spec = pl.BlockSpec((POOL_HALO, POOL_WIDTH), lambda r: (jnp.maximum(r * blocks_per_tile - 1, 0), 0))
    next_spec = pl.BlockSpec((POOL_HALO, POOL_WIDTH),
                             lambda r: (jnp.minimum((r + 1) * blocks_per_tile, last_block), 0))
    lat_tiles = LAT_ROWS // tm
    per_batch = SEQ // tm
    inv_spec = pl.BlockSpec((tm, POOL_WIDTH), lambda r: (jnp.where(r >= lat_tiles, per_batch, r % per_batch), 0))
    return pl.pallas_call(
        _merge_kernel,
        grid=(rows // tm,),
        in_specs=[
            _row_spec(tm, D_MODEL),
            _mod_spec(tm),
            _const_spec((1, D_MODEL)),
            _row_spec(tm, ATTN_WIDTH),
            _row_spec(tm, POOL_WIDTH),
            prev_spec,
            next_spec,
            inv_spec,
            _row_spec(tm, 2 * SGU_WIDTH),
            _const_spec((D_MODEL, 3 * D_MODEL)),
            _const_spec((POOL_WIDTH, POOL_WIDTH)),
            _const_spec((1, POOL_WIDTH)),
            _const_spec((1, SGU_WIDTH)),
            _const_spec((CHUNK, SGU_GROUPS * CHUNK)),
            _const_spec((CHUNK, SGU_WIDTH)),
            _layer_spec((POOL_WIDTH, D_MODEL), layer),
            _layer_spec((ATTN_WIDTH, D_MODEL), layer),
            _layer_spec((SGU_WIDTH, D_MODEL), layer),
            _layer_spec((D_MODEL, D_MODEL), layer),
        ],
        out_specs=_row_spec(tm, D_MODEL),
        out_shape=jax.ShapeDtypeStruct((rows, D_MODEL), F32),
        scratch_shapes=[
            pltpu.VMEM((SUB + 32, POOL_WIDTH), F32),
            pltpu.VMEM((SUB + 24, POOL_WIDTH), F32),
            pltpu.VMEM((SUB + 16, POOL_WIDTH), F32),
            pltpu.VMEM((SUB + 8, LANES), F32),
        ],
        compiler_params=_params(),
        name="merge",
    )(h, mod, g, attn, pool_in, pool_in, pool_in, pool_inv, sgu_in, wgate, wbd, pscale, snorm, wcat, sbias,
      wbp, wba, wbs, wout)


def _rope_tables():
    pos = jnp.arange(SEQ, dtype=jnp.int32)
    row = (pos // GRID_W).astype(F32)[:, None]
    col = (pos % GRID_W).astype(F32)[:, None]
    half = HEAD_DIM // 2
    inv_freq = ROPE_THETA ** (-jnp.arange(0, half, 2, dtype=F32) / half)[None, :]
    ang = jnp.concatenate([row * inv_freq, row * inv_freq, col * inv_freq, col * inv_freq], axis=1)
    sign = jnp.tile(jnp.concatenate([-jnp.ones((half // 2,), F32), jnp.ones((half // 2,), F32)]), 2)[None, :]
    cos = jnp.concatenate([jnp.cos(ang), jnp.ones((TM_PROJ, HEAD_DIM), F32)], axis=0)
    sin = jnp.concatenate([jnp.sin(ang) * sign, jnp.zeros((TM_PROJ, HEAD_DIM), F32)], axis=0)
    return jnp.tile(cos, (1, LANES // HEAD_DIM)), jnp.tile(sin, (1, LANES // HEAD_DIM))


def _pool_inverse_counts():
    half = jnp.repeat(jnp.array([w // 2 for w in POOL_WINDOWS], jnp.int32), POOL_GROUP_DIM)[None, :]

    def table(seg_len):
        pos = jnp.arange(seg_len, dtype=jnp.int32)[:, None]
        cnt = jnp.minimum(pos + half, seg_len) - jnp.maximum(pos - half, 0)
        return 1.0 / cnt.astype(F32)

    return jnp.concatenate([table(SEQ), jnp.tile(table(CTX_LEN), (TM_MERGE // CTX_LEN, 1))], axis=0)


def _block_diag(blocks):
    n, r, c = blocks.shape
    eye = jnp.eye(n, dtype=blocks.dtype)
    return (eye[:, None, :, None] * blocks[:, :, None, :]).reshape(n * r, n * c)


def kernel(x, c, ctx, c_ctx, w_ada, b_ada, norm_ffn1, ffn1_w13, ffn1_w2, norm_mix, w_in, q_norm, k_norm,
           pool_w, pool_scale, sgu_norm, sgu_w, sgu_b, w_br_pool, w_br_attn, w_br_sgu, w_out, norm_ffn2,
           ffn2_w13, ffn2_w2, final_norm):
    cond = jnp.concatenate([c, c_ctx[None, :], jnp.zeros((COND_ROWS - BATCH - 1, D_MODEL), F32)], axis=0)
    mods = _adaln(cond, w_ada, b_ada).reshape(DEPTH, COND_ROWS, N_MOD, D_MODEL)
    cos, sin = _rope_tables()
    gmat = _block_diag(jnp.full((LANES // HEAD_DIM, HEAD_DIM, HEAD_DIM), 1.0 / HEAD_DIM, F32)).astype(BF16)
    fg = final_norm.reshape(1, D_MODEL)
    pool_inv = _pool_inverse_counts()
    lat_tiles = LAT_ROWS // TM_PROJ
    ctx_tiles = CTX_ROWS // TM_PROJ

    ffn1_w13_bf, ffn1_w2_bf = ffn1_w13.astype(BF16), ffn1_w2.astype(BF16)
    ffn2_w13_bf, ffn2_w2_bf = ffn2_w13.astype(BF16), ffn2_w2.astype(BF16)
    w_br_pool_bf, w_br_attn_bf = w_br_pool.astype(BF16), w_br_attn.astype(BF16)
    w_br_sgu_bf, w_out_bf = w_br_sgu.astype(BF16), w_out.astype(BF16)

    hs = (x.reshape(LAT_ROWS, D_MODEL), ctx.reshape(CTX_ROWS, D_MODEL))
    for l in range(DEPTH):
        last = l == DEPTH - 1
        mod = mods[l]
        g_mix = norm_mix[l].reshape(1, D_MODEL)
        qg = jnp.tile(q_norm[l], LANES // HEAD_DIM).reshape(1, LANES)
        kg = jnp.tile(k_norm[l], LANES // HEAD_DIM).reshape(1, LANES)
        h = _ffn(hs, mod, norm_ffn1[l].reshape(1, D_MODEL), ffn1_w13_bf, ffn1_w2_bf,
                 fg, layer=l, mod_base=0, rows=ALL_ROWS, final=False)
        w_proj = w_in[l, :, :OFF_GATE].astype(BF16)
        if last:
            qt, k, vt, pool_in, sgu_in = _proj(h, mod, g_mix, w_proj, qg, kg, cos, sin, gmat,
                                               first_tile=0, n_tiles=lat_tiles, kv_only=False)
            k_ctx, vt_ctx = _proj(h, mod, g_mix, w_in[l, :, OFF_K:OFF_POOL].astype(BF16), None, kg, cos, sin, gmat,
                                  first_tile=lat_tiles, n_tiles=ctx_tiles, kv_only=True)
            attn = _attention(qt, k, vt, k_ctx, vt_ctx, ctx_block=0, with_ctx_queries=False)
            rows = LAT_ROWS
        else:
            qt, k, vt, pool_in, sgu_in = _proj(h, mod, g_mix, w_proj, qg, kg, cos, sin, gmat,
                                               first_tile=0, n_tiles=lat_tiles + ctx_tiles, kv_only=False)
            attn = _attention(qt, k, vt, k, vt, ctx_block=LAT_ROWS // CTX_LEN, with_ctx_queries=True)
            rows = ALL_ROWS
        h = _merge(
            h, mod, g_mix, attn, pool_in, pool_inv, sgu_in, w_in[l, :, OFF_GATE:].astype(BF16),
            _block_diag(pool_w[l]).astype(BF16), pool_scale[l].reshape(1, POOL_WIDTH),
            sgu_norm[l].reshape(1, SGU_WIDTH),
            jnp.transpose(sgu_w[l], (1, 0, 2)).reshape(CHUNK, SGU_GROUPS * CHUNK).astype(BF16),
            jnp.repeat(sgu_b[l].T, SGU_GROUP_DIM, axis=1),
            w_br_pool_bf, w_br_attn_bf, w_br_sgu_bf, w_out_bf, layer=l, rows=rows)
        hs = (_ffn((h,), mod, norm_ffn2[l].reshape(1, D_MODEL), ffn2_w13_bf,
                   ffn2_w2_bf, fg, layer=l, mod_base=6, rows=rows, final=last),)
    return hs[0].reshape(BATCH, SEQ, D_MODEL)
```

```python
import functools

import jax
import jax.numpy as jnp
from jax import lax
from jax.experimental import pallas as pl
from jax.experimental.pallas import tpu as pltpu

D_MODEL = 1024
BATCH = 16
SEQ = 2048
DEPTH = 2
GRID_W = 64
CTX_LEN = 256
N_HEADS = 8
N_KV_HEADS = 2
HEAD_DIM = 64
ATTN_WIDTH = N_HEADS * HEAD_DIM
KV_WIDTH = N_KV_HEADS * HEAD_DIM
ROPE_THETA = 10000.0
POOL_WINDOWS = (2, 4, 8, 16)
POOL_GROUPS = 4
POOL_GROUP_DIM = 64
POOL_WIDTH = POOL_GROUPS * POOL_GROUP_DIM
SGU_GROUPS = 4
SGU_GROUP_DIM = 64
SGU_WIDTH = SGU_GROUPS * SGU_GROUP_DIM
CHUNK = 128
D_FF = 2816
EPS = 1e-6
N_MOD = 9
OFF_K = ATTN_WIDTH
OFF_POOL = ATTN_WIDTH + 2 * KV_WIDTH
OFF_GATE = OFF_POOL + POOL_WIDTH + 2 * SGU_WIDTH

LAT_ROWS = BATCH * SEQ
CTX_ROWS = BATCH * CTX_LEN
ALL_ROWS = LAT_ROWS + CTX_ROWS
GROUP_HEADS = N_HEADS // N_KV_HEADS
LANES = 128
COND_ROWS = 24
POOL_HALO = max(POOL_WINDOWS) // 2
BF16_ROWS = 16
V_ROWS = HEAD_DIM + BF16_ROWS
VT_ROWS = N_KV_HEADS * V_ROWS
Q_SCALE = HEAD_DIM ** -0.5 * 1.4426950408889634
SUB = 256
TM_FFN = 1024
TM_PROJ = 2048
TM_MERGE = 1024
TQ = CTX_LEN
KEY_CHUNK = 256
VMEM_LIMIT = 56 * 1024 * 1024

F32 = jnp.float32
BF16 = jnp.bfloat16


def _dot(a, b):
    return jnp.dot(a, b, preferred_element_type=F32)


def _rms(x, g):
    return x * lax.rsqrt(jnp.mean(x * x, axis=-1, keepdims=True) + EPS) * g


def _modulate(x, g, mod_ref, base):
    shift = mod_ref[base:base + 1, :]
    scale = mod_ref[base + 1:base + 2, :]
    return _rms(x, g * (1.0 + scale)) + shift


def _const_spec(shape):
    zeros = (0,) * len(shape)
    return pl.BlockSpec(shape, lambda *_: zeros, pipeline_mode=pl.Buffered(1))


def _layer_spec(shape, layer):
    index = (layer,) + (0,) * len(shape)
    return pl.BlockSpec((None,) + tuple(shape), lambda *_: index, pipeline_mode=pl.Buffered(1))


def _row_spec(tm, width):
    return pl.BlockSpec((tm, width), lambda r: (r, 0))


def _col_spec(height, tm):
    return pl.BlockSpec((height, tm), lambda r: (0, r))


def _mod_spec(tm):
    lat_tiles = LAT_ROWS // tm
    per_batch = SEQ // tm
    return pl.BlockSpec((None, N_MOD, D_MODEL),
                        lambda r: (jnp.where(r >= lat_tiles, BATCH, r // per_batch), 0, 0))


def _params():
    return pltpu.CompilerParams(vmem_limit_bytes=VMEM_LIMIT)


def _software_pipeline(stages, n_sub):
    carry = [None] * n_sub
    for step in range(n_sub + len(stages) - 1):
        for s in reversed(range(len(stages))):
            j = step - s
            if 0 <= j < n_sub:
                carry[j] = stages[s](j, carry[j])


def _adaln_kernel(cond_ref, w_ref, b_ref, o_ref):
    cond = cond_ref[...]
    s = (cond * jax.nn.sigmoid(cond)).astype(BF16)
    o_ref[...] = _dot(s, w_ref[...].astype(BF16)) + b_ref[...]


def _adaln(cond, w_ada, b_ada):
    return pl.pallas_call(
        _adaln_kernel,
        grid=(DEPTH, N_MOD),
        in_specs=[
            pl.BlockSpec((COND_ROWS, D_MODEL), lambda l, j: (0, 0)),
            pl.BlockSpec((None, D_MODEL, D_MODEL), lambda l, j: (l, 0, j)),
            pl.BlockSpec((None, 1, D_MODEL), lambda l, j: (l, 0, j)),
        ],
        out_specs=pl.BlockSpec((None, COND_ROWS, D_MODEL), lambda l, j: (l, 0, j)),
        out_shape=jax.ShapeDtypeStruct((DEPTH, COND_ROWS, N_MOD * D_MODEL), F32),
        compiler_params=_params(),
        name="adaln",
    )(cond, w_ada, b_ada.reshape(DEPTH, 1, N_MOD * D_MODEL))


def _ffn_kernel(*refs, mod_base, final, split_input):
    if split_input:
        lat_ref, ctx_ref, mod_ref, g_ref, w13_ref, w2_ref, fg_ref, o_ref = refs
        is_lat = pl.program_id(0) < LAT_ROWS // TM_FFN
    else:
        h_ref, mod_ref, g_ref, w13_ref, w2_ref, fg_ref, o_ref = refs

    def load(j):
        rows = pl.ds(j * SUB, SUB)
        if split_input:
            return jnp.where(is_lat, lat_ref[rows, :], ctx_ref[rows, :])
        return h_ref[rows, :]

    def norm(j, _):
        return _modulate(load(j), g_ref[...], mod_ref, mod_base).astype(BF16)

    def up(j, xn):
        return _dot(xn, w13_ref[:, :D_FF]), _dot(xn, w13_ref[:, D_FF:])

    def act(j, ab):
        a, b = ab
        return (a * jax.nn.sigmoid(a) * b).astype(BF16)

    def down(j, u):
        return _dot(u, w2_ref[...])

    def store(j, y):
        out = load(j) + (0.5 * mod_ref[mod_base + 2:mod_base + 3, :]) * y
        if final:
            out = _rms(out, fg_ref[...])
        o_ref[pl.ds(j * SUB, SUB), :] = out

    _software_pipeline([norm, up, act, down, store], TM_FFN // SUB)


def _ffn(hs, mod, g, w13, w2, fg, *, layer, mod_base, rows, final):
    split_input = len(hs) == 2
    lat_tiles = LAT_ROWS // TM_FFN
    if split_input:
        h_specs = [pl.BlockSpec((TM_FFN, D_MODEL), lambda r: (jnp.minimum(r, lat_tiles - 1), 0)),
                   pl.BlockSpec((TM_FFN, D_MODEL), lambda r: (jnp.maximum(r - lat_tiles, 0), 0))]
    else:
        h_specs = [_row_spec(TM_FFN, D_MODEL)]
    return pl.pallas_call(
        functools.partial(_ffn_kernel, mod_base=mod_base, final=final, split_input=split_input),
        grid=(rows // TM_FFN,),
        in_specs=h_specs + [
            _mod_spec(TM_FFN),
            _const_spec((1, D_MODEL)),
            _layer_spec((D_MODEL, 2 * D_FF), layer),
            _layer_spec((D_FF, D_MODEL), layer),
            _const_spec((1, D_MODEL)),
        ],
        out_specs=_row_spec(TM_FFN, D_MODEL),
        out_shape=jax.ShapeDtypeStruct((rows, D_MODEL), F32),
        compiler_params=_params(),
        name="ffn",
    )(*hs, mod, g, w13, w2, fg)


def _head_norm_rope(x, gmat, g, cos, sin, first_half):
    ms = _dot((x * x).astype(BF16), gmat)
    y = x * lax.rsqrt(ms + EPS) * g
    up = pltpu.roll(y, LANES - HEAD_DIM // 4, 1)
    dn = pltpu.roll(y, HEAD_DIM // 4, 1)
    return y * cos + jnp.where(first_half, up, dn) * sin


def _proj_kernel(*refs, kv_only):
    if kv_only:
        h_ref, mod_ref, g_ref, w_ref, kg_ref, cos_ref, sin_ref, gmat_ref, k_ref, vt_ref = refs
    else:
        (h_ref, mod_ref, g_ref, w_ref, qg_ref, kg_ref, cos_ref, sin_ref, gmat_ref,
         qt_ref, k_ref, vt_ref, pool_ref, sgu_ref) = refs
    lane = lax.broadcasted_iota(jnp.int32, (1, LANES), 1)
    first_half = (lane % (HEAD_DIM // 2)) < (HEAD_DIM // 4)
    base = 0 if kv_only else OFF_K

    def norm(j, _):
        return _modulate(h_ref[pl.ds(j * SUB, SUB), :], g_ref[...], mod_ref, 3).astype(BF16)

    def project(j, hn):
        return _dot(hn, w_ref[...])

    def finish(j, p):
        rows = pl.ds(j * SUB, SUB)
        gmat, cos, sin = gmat_ref[...], cos_ref[rows, :], sin_ref[rows, :]
        if not kv_only:
            for c in range(ATTN_WIDTH // LANES):
                qc = _head_norm_rope(p[:, c * LANES:(c + 1) * LANES], gmat, qg_ref[...], cos, sin, first_half)
                qt_ref[c * LANES:(c + 1) * LANES, rows] = qc.T.astype(BF16)
            pool_ref[rows, :] = p[:, OFF_POOL:OFF_POOL + POOL_WIDTH]
            sgu_ref[rows, :] = p[:, OFF_POOL + POOL_WIDTH:OFF_GATE]
        k = _head_norm_rope(p[:, base:base + KV_WIDTH], gmat, kg_ref[...], cos, sin, first_half)
        k_ref[rows, :] = k.astype(BF16)
        vt = p[:, base + KV_WIDTH:base + 2 * KV_WIDTH].T.astype(BF16)
        ones = jnp.ones((V_ROWS - HEAD_DIM, SUB), BF16)
        for g in range(N_KV_HEADS):
            vt_ref[g * V_ROWS:g * V_ROWS + HEAD_DIM, rows] = vt[g * HEAD_DIM:(g + 1) * HEAD_DIM, :]
            vt_ref[g * V_ROWS + HEAD_DIM:(g + 1) * V_ROWS, rows] = ones

    _software_pipeline([norm, project, finish], TM_PROJ // SUB)


def _proj(h, mod, g, w, qg, kg, cos, sin, gmat, *, first_tile, n_tiles, kv_only):
    tm = TM_PROJ
    lat_tiles = LAT_ROWS // tm
    per_batch = SEQ // tm
    rows = n_tiles * tm
    h_spec = pl.BlockSpec((tm, D_MODEL), lambda r: (r + first_tile, 0))
    mod_spec = pl.BlockSpec(
        (None, N_MOD, D_MODEL),
        lambda r: (jnp.where(r + first_tile >= lat_tiles, BATCH, (r + first_tile) // per_batch), 0, 0))
    tab_spec = pl.BlockSpec(
        (tm, LANES), lambda r: (jnp.where(r + first_tile >= lat_tiles, per_batch, (r + first_tile) % per_batch), 0))
    gains = [_const_spec((1, LANES))] * (1 if kv_only else 2)
    in_specs = [h_spec, mod_spec, _const_spec((1, D_MODEL)), _const_spec(w.shape)] + gains + [
        tab_spec, tab_spec, _const_spec((LANES, LANES))]
    out_specs = [_row_spec(tm, KV_WIDTH), _col_spec(VT_ROWS, tm)]
    out_shape = [jax.ShapeDtypeStruct((rows, KV_WIDTH), BF16), jax.ShapeDtypeStruct((VT_ROWS, rows), BF16)]
    if not kv_only:
        out_specs = [_col_spec(ATTN_WIDTH, tm)] + out_specs + [_row_spec(tm, POOL_WIDTH), _row_spec(tm, 2 * SGU_WIDTH)]
        out_shape = ([jax.ShapeDtypeStruct((ATTN_WIDTH, rows), BF16)] + out_shape
                     + [jax.ShapeDtypeStruct((rows, POOL_WIDTH), F32),
                        jax.ShapeDtypeStruct((rows, 2 * SGU_WIDTH), F32)])
    args = (h, mod, g, w) + ((kg,) if kv_only else (qg, kg)) + (cos, sin, gmat)
    return pl.pallas_call(
        functools.partial(_proj_kernel, kv_only=kv_only),
        grid=(n_tiles,),
        in_specs=in_specs,
        out_specs=out_specs,
        out_shape=out_shape,
        compiler_params=_params(),
        name="proj_kv" if kv_only else "proj",
    )(*args)


def _attend(qt_ref, k_refs, vt_refs, o_ref):
    zeros = jnp.zeros((HEAD_DIM, TQ), BF16)
    w = []
    for h in range(N_HEADS):
        qh = qt_ref[h * HEAD_DIM:(h + 1) * HEAD_DIM, :]
        w.append(jnp.concatenate([qh, zeros] if h < GROUP_HEADS else [zeros, qh], axis=0))

    chunks = [(k_ref, vt_ref, c * KEY_CHUNK, min(KEY_CHUNK, k_ref.shape[0] - c * KEY_CHUNK))
              for k_ref, vt_ref in zip(k_refs, vt_refs)
              for c in range(pl.cdiv(k_ref.shape[0], KEY_CHUNK))]

    def scores(chunk, h):
        k_ref, _, start, size = chunk
        return _dot(k_ref[start:start + size, :], w[h])

    m = [None] * N_HEADS
    acc = [None] * N_HEADS
    s_next = [scores(chunks[0], h) for h in range(N_HEADS)]
    for c, (k_ref, vt_ref, start, size) in enumerate(chunks):
        s = s_next
        s_next = [None] * N_HEADS
        for h in range(N_HEADS):
            if c + 1 < len(chunks):
                s_next[h] = scores(chunks[c + 1], h)
            m_chunk = jnp.max(s[h], axis=0, keepdims=True)
            m_new = m_chunk if c == 0 else jnp.maximum(m[h], m_chunk)
            p = jnp.exp2(s[h] - m_new).astype(BF16)
            base = (h // GROUP_HEADS) * V_ROWS
            pv = _dot(vt_ref[base:base + V_ROWS, start:start + size], p)
            acc[h] = pv if c == 0 else acc[h] * jnp.exp2(m[h] - m_new) + pv
            m[h] = m_new
    outs = [a[:HEAD_DIM, :] / a[HEAD_DIM:HEAD_DIM + 1, :] for a in acc]
    o_ref[...] = jnp.concatenate(outs, axis=0).T.astype(BF16)


def _attn_kernel(qt_ref, kl_ref, kc_ref, vtl_ref, vtc_ref, o_ref, *, with_ctx_queries):
    if with_ctx_queries:
        i = pl.program_id(1)

        @pl.when(i < SEQ // TQ)
        def _():
            _attend(qt_ref, [kl_ref, kc_ref], [vtl_ref, vtc_ref], o_ref)

        @pl.when(i == SEQ // TQ)
        def _():
            _attend(qt_ref, [kc_ref], [vtc_ref], o_ref)
    else:
        _attend(qt_ref, [kl_ref, kc_ref], [vtl_ref, vtc_ref], o_ref)


def _attention(qt, k_lat, vt_lat, k_ctx, vt_ctx, *, ctx_block, with_ctx_queries):
    per_batch = SEQ // TQ
    n_q = per_batch + (1 if with_ctx_queries else 0)
    rows = ALL_ROWS if with_ctx_queries else LAT_ROWS

    def q_block(b, i):
        return jnp.where(i < per_batch, b * per_batch + i, LAT_ROWS // TQ + b)

    return pl.pallas_call(
        functools.partial(_attn_kernel, with_ctx_queries=with_ctx_queries),
        grid=(BATCH, n_q),
        in_specs=[
            pl.BlockSpec((ATTN_WIDTH, TQ), lambda b, i: (0, q_block(b, i))),
            pl.BlockSpec((SEQ, KV_WIDTH), lambda b, i: (b, 0)),
            pl.BlockSpec((CTX_LEN, KV_WIDTH), lambda b, i: (ctx_block + b, 0)),
            pl.BlockSpec((VT_ROWS, SEQ), lambda b, i: (0, b)),
            pl.BlockSpec((VT_ROWS, CTX_LEN), lambda b, i: (0, ctx_block + b)),
        ],
        out_specs=pl.BlockSpec((TQ, ATTN_WIDTH), lambda b, i: (q_block(b, i), 0)),
        out_shape=jax.ShapeDtypeStruct((rows, ATTN_WIDTH), BF16),
        compiler_params=_params(),
        name="attention",
    )(qt, k_lat, k_ctx, vt_lat, vt_ctx)


def _pool_branch(j, pool_ref, prev_ref, next_ref, inv_ref, xe_ref, q2_ref, q4_ref, q8_ref):
    tm = SUB
    n_sub = TM_MERGE // SUB
    sub = pl.program_id(0) * n_sub + j
    is_ctx = sub >= LAT_ROWS // SUB
    per_seg = jnp.where(is_ctx, CTX_LEN // SUB, SEQ // SUB)
    seg_idx = jnp.where(is_ctx, sub - LAT_ROWS // SUB, sub) % per_seg
    first = seg_idx == 0
    last = seg_idx == per_seg - 1
    x = pool_ref[pl.ds(j * SUB, SUB), :]
    prev_rows = prev_ref[...] if j == 0 else pool_ref[pl.ds(j * SUB - POOL_HALO, POOL_HALO), :]
    next_rows = next_ref[...] if j == n_sub - 1 else pool_ref[pl.ds((j + 1) * SUB, POOL_HALO), :]
    xe_ref[0:POOL_HALO, :] = jnp.where(first, 0.0, prev_rows)
    xe_ref[POOL_HALO:POOL_HALO + tm, :] = x
    xe_ref[POOL_HALO + tm:2 * POOL_HALO + tm, :] = jnp.where(last, 0.0, next_rows)
    xe_ref[2 * POOL_HALO + tm:, :] = jnp.zeros((2 * POOL_HALO, POOL_WIDTH), F32)
    q2_ref[...] = xe_ref[0:tm + 24, :] + xe_ref[1:tm + 25, :]
    q4_ref[...] = q2_ref[0:tm + 16, :] + q2_ref[2:tm + 18, :]
    q8_ref[...] = q4_ref[0:tm + 8, LANES:] + q4_ref[4:tm + 12, LANES:]
    q16 = q8_ref[0:tm, :] + q8_ref[8:tm + 8, :]
    lane = lax.broadcasted_iota(jnp.int32, (1, LANES), 1)
    low = lane < POOL_GROUP_DIM
    sums = jnp.concatenate([
        jnp.where(low, q2_ref[7:7 + tm, 0:LANES], q4_ref[6:6 + tm, 0:LANES]),
        jnp.where(low, q8_ref[4:4 + tm, :], q16),
    ], axis=1)
    return (sums * inv_ref[pl.ds(j * SUB, SUB), :] - x).astype(BF16)


def _gelu_tanh(x):
    c = 0.7978845608028654
    half = 0.5 * x
    return half + half * jnp.tanh(x * (c + (c * 0.044715) * (x * x)))


def _sgu_prepare(j, sgu_ref, norm_ref):
    gl = _gelu_tanh(sgu_ref[pl.ds(j * SUB, SUB), :])
    u = gl[:, :SGU_WIDTH]
    v = _rms(gl[:, SGU_WIDTH:], norm_ref[...])
    lane = lax.broadcasted_iota(jnp.int32, (1, SGU_WIDTH), 1)
    stacks = []
    for c in range(SUB // CHUNK):
        vc = v[c * CHUNK:(c + 1) * CHUNK, :]
        stacks.append(jnp.concatenate(
            [jnp.where((lane >= g * SGU_GROUP_DIM) & (lane < (g + 1) * SGU_GROUP_DIM), vc, 0.0)
             for g in range(SGU_GROUPS)], axis=0).astype(BF16))
    return u, stacks


def _merge_kernel(h_ref, mod_ref, g_ref, attn_ref, pool_ref, prev_ref, next_ref, inv_ref, sgu_ref,
                  wgate_ref, wbd_ref, pscale_ref, snorm_ref, wcat_ref, sbias_ref,
                  wbp_ref, wba_ref, wbs_ref, wout_ref, o_ref,
                  xe_ref, q2_ref, q4_ref, q8_ref):
    def prepare(j, _):
        hn = _modulate(h_ref[pl.ds(j * SUB, SUB), :], g_ref[...], mod_ref, 3).astype(BF16)
        pooled = _pool_branch(j, pool_ref, prev_ref, next_ref, inv_ref, xe_ref, q2_ref, q4_ref, q8_ref)
        u, stacks = _sgu_prepare(j, sgu_ref, snorm_ref)
        return hn, pooled, u, stacks

    def gates(j, carry):
        hn, pooled, u, stacks = carry
        return [_dot(hn, wgate_ref[:, i * D_MODEL:(i + 1) * D_MODEL]) for i in range(3)], pooled, u, stacks

    def branches(j, carry):
        gate_logits, pooled, u, stacks = carry
        pool_out = (_dot(pooled, wbd_ref[...]) * pscale_ref[...]).astype(BF16)
        spatial = jnp.concatenate([_dot(wcat_ref[...], st) + sbias_ref[...] for st in stacks], axis=0)
        sgu_out = (u * spatial).astype(BF16)
        merged = jax.nn.sigmoid(gate_logits[0]) * _dot(pool_out, wbp_ref[...])
        merged += jax.nn.sigmoid(gate_logits[1]) * _dot(attn_ref[pl.ds(j * SUB, SUB), :], wba_ref[...])
        merged += jax.nn.sigmoid(gate_logits[2]) * _dot(sgu_out, wbs_ref[...])
        return merged.astype(BF16)

    def store(j, merged):
        rows = pl.ds(j * SUB, SUB)
        o_ref[rows, :] = h_ref[rows, :] + mod_ref[5:6, :] * _dot(merged, wout_ref[...])

    _software_pipeline([prepare, gates, branches, store], TM_MERGE // SUB)


def _merge(h, mod, g, attn, pool_in, pool_inv, sgu_in, wgate, wbd, pscale, snorm, wcat, sbias, wbp, wba, wbs, wout,
           *, layer, rows):
    tm = TM_MERGE
    blocks_per_tile = tm // POOL_HALO
    last_block = rows // POOL_HALO - 1
    prev_spec = pl.BlockSpec((POOL_HALO, POOL_WIDTH), lambda r: (jnp.maximum(r * blocks_per_tile - 1, 0), 0))
    next_spec = pl.BlockSpec((POOL_HALO, POOL_WIDTH),
                             lambda r: (jnp.minimum((r + 1) * blocks_per_tile, last_block), 0))
    lat_tiles = LAT_ROWS // tm
    per_batch = SEQ // tm
    inv_spec = pl.BlockSpec((tm, POOL_WIDTH), lambda r: (jnp.where(r >= lat_tiles, per_batch, r % per_batch), 0))
    return pl.pallas_call(
        _merge_kernel,
        grid=(rows // tm,),
        in_specs=[
            _row_spec(tm, D_MODEL),
            _mod_spec(tm),
            _const_spec((1, D_MODEL)),
            _row_spec(tm, ATTN_WIDTH),
            _row_spec(tm, POOL_WIDTH),
            prev_spec,
            next_spec,
            inv_spec,
            _row_spec(tm, 2 * SGU_WIDTH),
            _const_spec((D_MODEL, 3 * D_MODEL)),
            _const_spec((POOL_WIDTH, POOL_WIDTH)),
            _const_spec((1, POOL_WIDTH)),
            _const_spec((1, SGU_WIDTH)),
            _const_spec((CHUNK, SGU_GROUPS * CHUNK)),
            _const_spec((CHUNK, SGU_WIDTH)),
            _layer_spec((POOL_WIDTH, D_MODEL), layer),
            _layer_spec((ATTN_WIDTH, D_MODEL), layer),
            _layer_spec((SGU_WIDTH, D_MODEL), layer),
            _layer_spec((D_MODEL, D_MODEL), layer),
        ],
        out_specs=_row_spec(tm, D_MODEL),
        out_shape=jax.ShapeDtypeStruct((rows, D_MODEL), F32),
        scratch_shapes=[
            pltpu.VMEM((SUB + 32, POOL_WIDTH), F32),
            pltpu.VMEM((SUB + 24, POOL_WIDTH), F32),
            pltpu.VMEM((SUB + 16, POOL_WIDTH), F32),
            pltpu.VMEM((SUB + 8, LANES), F32),
        ],
        compiler_params=_params(),
        name="merge",
    )(h, mod, g, attn, pool_in, pool_in, pool_in, pool_inv, sgu_in, wgate, wbd, pscale, snorm, wcat, sbias,
      wbp, wba, wbs, wout)


def _rope_tables():
    pos = jnp.arange(SEQ, dtype=jnp.int32)
    row = (pos // GRID_W).astype(F32)[:, None]
    col = (pos % GRID_W).astype(F32)[:, None]
    half = HEAD_DIM // 2
    inv_freq = ROPE_THETA ** (-jnp.arange(0, half, 2, dtype=F32) / half)[None, :]
    ang = jnp.concatenate([row * inv_freq, row * inv_freq, col * inv_freq, col * inv_freq], axis=1)
    sign = jnp.tile(jnp.concatenate([-jnp.ones((half // 2,), F32), jnp.ones((half // 2,), F32)]), 2)[None, :]
    cos = jnp.concatenate([jnp.cos(ang), jnp.ones((TM_PROJ, HEAD_DIM), F32)], axis=0)
    sin = jnp.concatenate([jnp.sin(ang) * sign, jnp.zeros((TM_PROJ, HEAD_DIM), F32)], axis=0)
    return jnp.tile(cos, (1, LANES // HEAD_DIM)), jnp.tile(sin, (1, LANES // HEAD_DIM))


def _pool_inverse_counts():
    half = jnp.repeat(jnp.array([w // 2 for w in POOL_WINDOWS], jnp.int32), POOL_GROUP_DIM)[None, :]

    def table(seg_len):
        pos = jnp.arange(seg_len, dtype=jnp.int32)[:, None]
        cnt = jnp.minimum(pos + half, seg_len) - jnp.maximum(pos - half, 0)
        return 1.0 / cnt.astype(F32)

    return jnp.concatenate([table(SEQ), jnp.tile(table(CTX_LEN), (TM_MERGE // CTX_LEN, 1))], axis=0)


def _block_diag(blocks):
    n, r, c = blocks.shape
    eye = jnp.eye(n, dtype=blocks.dtype)
    return (eye[:, None, :, None] * blocks[:, :, None, :]).reshape(n * r, n * c)


def kernel(x, c, ctx, c_ctx, w_ada, b_ada, norm_ffn1, ffn1_w13, ffn1_w2, norm_mix, w_in, q_norm, k_norm,
           pool_w, pool_scale, sgu_norm, sgu_w, sgu_b, w_br_pool, w_br_attn, w_br_sgu, w_out, norm_ffn2,
           ffn2_w13, ffn2_w2, final_norm):
    cond = jnp.concatenate([c, c_ctx[None, :], jnp.zeros((COND_ROWS - BATCH - 1, D_MODEL), F32)], axis=0)
    mods = _adaln(cond, w_ada, b_ada).reshape(DEPTH, COND_ROWS, N_MOD, D_MODEL)
    cos, sin = _rope_tables()
    gmat = _block_diag(jnp.full((LANES // HEAD_DIM, HEAD_DIM, HEAD_DIM), 1.0 / HEAD_DIM, F32)).astype(BF16)
    fg = final_norm.reshape(1, D_MODEL)
    pool_inv = _pool_inverse_counts()
    lat_tiles = LAT_ROWS // TM_PROJ
    ctx_tiles = CTX_ROWS // TM_PROJ

    ffn1_w13_bf, ffn1_w2_bf = ffn1_w13.astype(BF16), ffn1_w2.astype(BF16)
    ffn2_w13_bf, ffn2_w2_bf = ffn2_w13.astype(BF16), ffn2_w2.astype(BF16)
    w_br_pool_bf, w_br_attn_bf = w_br_pool.astype(BF16), w_br_attn.astype(BF16)
    w_br_sgu_bf, w_out_bf = w_br_sgu.astype(BF16), w_out.astype(BF16)

    hs = (x.reshape(LAT_ROWS, D_MODEL), ctx.reshape(CTX_ROWS, D_MODEL))
    for l in range(DEPTH):
        last = l == DEPTH - 1
        mod = mods[l]
        g_mix = norm_mix[l].reshape(1, D_MODEL)
        qg = jnp.tile(q_norm[l], LANES // HEAD_DIM).reshape(1, LANES) * Q_SCALE
        kg = jnp.tile(k_norm[l], LANES // HEAD_DIM).reshape(1, LANES)
        h = _ffn(hs, mod, norm_ffn1[l].reshape(1, D_MODEL), ffn1_w13_bf, ffn1_w2_bf,
                 fg, layer=l, mod_base=0, rows=ALL_ROWS, final=False)
        w_proj = w_in[l, :, :OFF_GATE].astype(BF16)
        if last:
            qt, k, vt, pool_in, sgu_in = _proj(h, mod, g_mix, w_proj, qg, kg, cos, sin, gmat,
                                               first_tile=0, n_tiles=lat_tiles, kv_only=False)
            k_ctx, vt_ctx = _proj(h, mod, g_mix, w_in[l, :, OFF_K:OFF_POOL].astype(BF16), None, kg, cos, sin, gmat,
                                  first_tile=lat_tiles, n_tiles=ctx_tiles, kv_only=True)
            attn = _attention(qt, k, vt, k_ctx, vt_ctx, ctx_block=0, with_ctx_queries=False)
            rows = LAT_ROWS
        else:
            qt, k, vt, pool_in, sgu_in = _proj(h, mod, g_mix, w_proj, qg, kg, cos, sin, gmat,
                                               first_tile=0, n_tiles=lat_tiles + ctx_tiles, kv_only=False)
            attn = _attention(qt, k, vt, k, vt, ctx_block=LAT_ROWS // CTX_LEN, with_ctx_queries=True)
            rows = ALL_ROWS
        h = _merge(
            h, mod, g_mix, attn, pool_in, pool_inv, sgu_in, w_in[l, :, OFF_GATE:].astype(BF16),
            _block_diag(pool_w[l]).astype(BF16), pool_scale[l].reshape(1, POOL_WIDTH),
            sgu_norm[l].reshape(1, SGU_WIDTH),
            jnp.transpose(sgu_w[l], (1, 0, 2)).reshape(CHUNK, SGU_GROUPS * CHUNK).astype(BF16),
            jnp.repeat(sgu_b[l].T, SGU_GROUP_DIM, axis=1),
            w_br_pool_bf, w_br_attn_bf, w_br_sgu_bf, w_out_bf, layer=l, rows=rows)
        hs = (_ffn((h,), mod, norm_ffn2[l].reshape(1, D_MODEL), ffn2_w13_bf,
                   ffn2_w2_bf, fg, layer=l, mod_base=6, rows=rows, final=last),)
    return hs[0].reshape(BATCH, SEQ, D_MODEL)
```
